```python
import math
import functools
import jax
import jax.numpy as jnp
from jax import lax
import numpy as np

D_MODEL = 1024
BATCH = 8
SEQ = 2048
DEPTH = 1
DEC_BATCH = 128
DEC_SEQ = 4
PAST_LEN = 8192
PAGE_SIZE = 128

MIX_W = D_MODEL
ATT_W = MIX_W // 2
RW_W = MIX_W - ATT_W
ATT_HEADS = 4
ATT_QK_DIM = ATT_W // (2 * ATT_HEADS)
ATT_V_DIM = 2 * ATT_QK_DIM
RW_HEAD = 64
RW_HEADS = RW_W // RW_HEAD
DECAY_LORA = 64
AAA_LORA = 64
GATE_LORA = 128
RW_PROJ_W = 3 * RW_W + DECAY_LORA + AAA_LORA + GATE_LORA
IN_W = 3 * ATT_W + RW_PROJ_W
D_FF = -(-8 * D_MODEL // (3 * 256)) * 256
Q_BLOCK = 128
RMS_EPS = 1e-6
SUBLN_EPS = 1e-5
GN_EPS = 64e-5

kernel_name = "hymba_diffattn_rwkv7_sandwich_step"


def _rms_norm(x, g, eps=RMS_EPS):
    xf = x.astype(jnp.float32)
    y = xf * lax.rsqrt(jnp.mean(xf * xf, axis=-1, keepdims=True) + eps)
    return (y * g.astype(jnp.float32)).astype(x.dtype)


def _diff_lambda(lq1, lk1, lq2, lk2, lam_init):
    f = jnp.float32
    return (jnp.exp(jnp.sum(lq1.astype(f) * lk1.astype(f)))
            - jnp.exp(jnp.sum(lq2.astype(f) * lk2.astype(f))) + lam_init)


def _diff_weights(s, lam, dtype):
    p = jax.nn.softmax(s, axis=-1)
    return (p[..., 0, :, :] - lam * p[..., 1, :, :]).astype(dtype)


def _attend_prompt(q, k, v, lam):
    B, S = q.shape[:2]
    nb = S // Q_BLOCK
    qb = q.reshape(B, nb, Q_BLOCK, ATT_HEADS, 2, ATT_QK_DIM).transpose(1, 0, 2, 3, 4, 5)
    k_pos = jnp.arange(S)
    scale = ATT_QK_DIM ** -0.5

    def block(args):
        qi, i = args
        s = jnp.einsum('bqhcd,bkhcd->bhcqk', qi, k).astype(jnp.float32) * scale
        q_pos = i * Q_BLOCK + jnp.arange(Q_BLOCK)
        mask = q_pos[:, None] >= k_pos[None, :]
        s = jnp.where(mask, s, -jnp.inf)
        a = _diff_weights(s, lam, v.dtype)
        return jnp.einsum('bhqk,bkhe->bqhe', a, v)

    o = lax.map(block, (qb, jnp.arange(nb)))
    return o.transpose(1, 0, 2, 3, 4).reshape(B, S, ATT_HEADS, ATT_V_DIM)


def _attend_paged(cache_k, cache_v, page_table, layer, q, k, v, lam):
    scale = ATT_QK_DIM ** -0.5

    def one(args):
        qs, ks, vs, pages = args
        kp = cache_k[layer, pages].reshape(-1, ATT_HEADS, 2, ATT_QK_DIM).astype(ks.dtype)
        vp = cache_v[layer, pages].reshape(-1, ATT_HEADS, ATT_V_DIM).astype(vs.dtype)
        past = kp.shape[0]
        T = qs.shape[0]
        keys = jnp.concatenate([kp, ks], axis=0)
        vals = jnp.concatenate([vp, vs], axis=0)
        s = jnp.einsum('qhcd,khcd->hcqk', qs, keys).astype(jnp.float32) * scale
        k_idx = jnp.arange(past + T)
        q_idx = past + jnp.arange(T)
        mask = k_idx[None, :] <= q_idx[:, None]
        s = jnp.where(mask, s, -jnp.inf)
        a = _diff_weights(s, lam, vals.dtype)
        return jnp.einsum('hqk,khe->qhe', a, vals)

    return lax.map(one, (q, k, v, page_table))


def _wkv7_scan(r, decay, k, v, kk, a, s0):
    xs = tuple(t.swapaxes(0, 1) for t in (r, decay, k, v, kk, a))

    def step(S, inp):
        r_t, w_t, k_t, v_t, kk_t, a_t = inp
        sk = jnp.einsum('bhij,bhj->bhi', S, kk_t)
        S = (S * w_t[:, :, None, :] - sk[..., None] * (kk_t * a_t)[:, :, None, :]
             + v_t[..., None] * k_t[:, :, None, :])
        y = jnp.einsum('bhij,bhj->bhi', S, r_t)
        return S, y

    S, y = lax.scan(step, s0, xs)
    return y.swapaxes(0, 1), S


def _rwkv7_mixer(rw, shift0, wkv0, p):
    B, T, _ = rw.shape
    f32 = jnp.float32
    prev = jnp.concatenate([shift0[:, None, :].astype(rw.dtype), rw[:, :-1]], axis=1)
    xs = (rw + (prev - rw) * p['rw_mu']).astype(f32)
    r = xs[..., :RW_W]
    k = xs[..., RW_W:2 * RW_W]
    v = xs[..., 2 * RW_W:3 * RW_W]
    o = 3 * RW_W
    w_lo = xs[..., o:o + DECAY_LORA]
    a_lo = xs[..., o + DECAY_LORA:o + DECAY_LORA + AAA_LORA]
    g_lo = xs[..., o + DECAY_LORA + AAA_LORA:]
    w_raw = -jax.nn.softplus(-(p['rw_w0'].astype(f32) + jnp.tanh(w_lo) @ p['rw_w2'].astype(f32))) - 0.5
    decay = jnp.exp(-jnp.exp(w_raw))
    a = jax.nn.sigmoid(p['rw_a0'].astype(f32) + a_lo @ p['rw_a2'].astype(f32))
    g = jax.nn.sigmoid(g_lo) @ p['rw_g2'].astype(f32)

    def heads(t):
        return t.reshape(B, T, RW_HEADS, RW_HEAD)

    kk = heads(k * p['rw_kk'].astype(f32))
    kk = kk / jnp.maximum(jnp.sqrt(jnp.sum(kk * kk, axis=-1, keepdims=True)), 1e-12)
    k = k * (1.0 + (a - 1.0) * p['rw_ka'].astype(f32))
    r, k, v, a, decay = heads(r), heads(k), heads(v), heads(a), heads(decay)
    y, s_new = _wkv7_scan(r, decay, k, v, kk, a, wkv0.astype(f32))
    mean = jnp.mean(y, axis=-1, keepdims=True)
    var = jnp.mean(jnp.square(y - mean), axis=-1, keepdims=True)
    yn = ((y - mean) * lax.rsqrt(var + GN_EPS)).reshape(B, T, RW_W)
    yn = yn * p['rw_lnx_w'].astype(f32) + p['rw_lnx_b'].astype(f32)
    bonus = (jnp.sum(r * k * p['rw_rk'].astype(f32), axis=-1, keepdims=True) * v).reshape(B, T, RW_W)
    out = (yn + bonus) * g
    return out.astype(rw.dtype), s_new.astype(wkv0.dtype), rw[:, -1]


def _layer(x, shift0, wkv0, attend, p, lam_init):
    B, T, _ = x.shape
    h = _rms_norm(x, p['norm_mix_pre'])
    z = h @ p['w_in']
    q = z[..., :ATT_W].reshape(B, T, ATT_HEADS, 2, ATT_QK_DIM)
    k = z[..., ATT_W:2 * ATT_W].reshape(B, T, ATT_HEADS, 2, ATT_QK_DIM)
    v = z[..., 2 * ATT_W:3 * ATT_W].reshape(B, T, ATT_HEADS, ATT_V_DIM)
    rw = z[..., 3 * ATT_W:]
    lam = _diff_lambda(p['lambda_q1'], p['lambda_k1'], p['lambda_q2'], p['lambda_k2'], lam_init)
    att = attend(q, k, v, lam)
    att = _rms_norm(att, p['subln_w'], SUBLN_EPS) * (1.0 - lam_init)
    rw_out, wkv_new, shift_new = _rwkv7_mixer(rw, shift0, wkv0, p)
    mix = jnp.concatenate([att.reshape(B, T, ATT_W), rw_out], axis=-1) @ p['w_out']
    x = x + _rms_norm(mix, p['norm_mix_post'])
    f = _rms_norm(x, p['norm_ffn_pre'])
    f = (jax.nn.silu(f @ p['ffn_gate']) * (f @ p['ffn_up'])) @ p['ffn_down']
    x = x + _rms_norm(f, p['norm_ffn_post'])
    return x, k.reshape(B, T, ATT_HEADS, 2 * ATT_QK_DIM), v, wkv_new, shift_new


def setup_inputs(seed: int = 0) -> dict:
    key = jax.random.key(seed)
    keys = iter(jax.random.split(key, 48))
    f = jnp.float32
    n_pages = PAST_LEN // PAGE_SIZE
    n_pool = (DEC_BATCH * n_pages * 5) // 4

    def nrm(shape, scale):
        return jax.random.normal(next(keys), shape, f) * scale

    def unif(shape, lo, hi):
        return jax.random.uniform(next(keys), shape, f, lo, hi)

    def gain(shape):
        return 1.0 + nrm(shape, 0.05)

    L = DEPTH
    page_table = jax.random.permutation(next(keys), n_pool)[:DEC_BATCH * n_pages]
    page_table = page_table.reshape(DEC_BATCH, n_pages).astype(jnp.int32)
    return {
        'x_prompt': nrm((BATCH, SEQ, D_MODEL), 1.0),
        'x_sample': nrm((DEC_BATCH, DEC_SEQ, D_MODEL), 1.0),
        'cache_k': nrm((L, n_pool, PAGE_SIZE, ATT_HEADS, 2 * ATT_QK_DIM), 1.0),
        'cache_v': nrm((L, n_pool, PAGE_SIZE, ATT_HEADS, ATT_V_DIM), 1.0),
        'state_wkv': nrm((L, DEC_BATCH, RW_HEADS, RW_HEAD, RW_HEAD), 0.5),
        'state_shift': nrm((L, DEC_BATCH, RW_PROJ_W), 1.0),
        'page_table': page_table,
        'norm_mix_pre': gain((L, D_MODEL)),
        'norm_mix_post': gain((L, D_MODEL)),
        'norm_ffn_pre': gain((L, D_MODEL)),
        'norm_ffn_post': gain((L, D_MODEL)),
        'w_in': nrm((L, D_MODEL, IN_W), D_MODEL ** -0.5),
        'w_out': nrm((L, MIX_W, D_MODEL), MIX_W ** -0.5),
        'lambda_q1': nrm((L, ATT_QK_DIM), 0.1),
        'lambda_k1': nrm((L, ATT_QK_DIM), 0.1),
        'lambda_q2': nrm((L, ATT_QK_DIM), 0.1),
        'lambda_k2': nrm((L, ATT_QK_DIM), 0.1),
        'subln_w': gain((L, ATT_V_DIM)),
        'rw_mu': unif((L, RW_PROJ_W), 0.0, 1.0),
        'rw_w0': unif((L, RW_W), -4.0, 0.0),
        'rw_w2': nrm((L, DECAY_LORA, RW_W), 0.5 * DECAY_LORA ** -0.5),
        'rw_a0': nrm((L, RW_W), 0.1),
        'rw_a2': nrm((L, AAA_LORA, RW_W), 0.5 * AAA_LORA ** -0.5),
        'rw_g2': nrm((L, GATE_LORA, RW_W), GATE_LORA ** -0.5),
        'rw_kk': 0.85 + nrm((L, RW_W), 0.05),
        'rw_ka': gain((L, RW_W)),
        'rw_rk': nrm((L, RW_HEADS, RW_HEAD), 0.1),
        'rw_lnx_w': gain((L, RW_W)),
        'rw_lnx_b': nrm((L, RW_W), 0.02),
        'ffn_gate': nrm((L, D_MODEL, D_FF), D_MODEL ** -0.5),
        'ffn_up': nrm((L, D_MODEL, D_FF), D_MODEL ** -0.5),
        'ffn_down': nrm((L, D_FF, D_MODEL), D_FF ** -0.5),
    }


def reference(x_prompt, x_sample, cache_k, cache_v, state_wkv, state_shift, page_table,
              norm_mix_pre, norm_mix_post, norm_ffn_pre, norm_ffn_post, w_in, w_out,
              lambda_q1, lambda_k1, lambda_q2, lambda_k2, subln_w,
              rw_mu, rw_w0, rw_w2, rw_a0, rw_a2, rw_g2, rw_kk, rw_ka, rw_rk, rw_lnx_w, rw_lnx_b,
              ffn_gate, ffn_up, ffn_down):
    yp, ys = x_prompt, x_sample
    kp_l, vp_l, wp_l, sp_l = [], [], [], []
    ks_l, vs_l, ws_l, ss_l = [], [], [], []
    for l in range(DEPTH):
        lam_init = 0.8 - 0.6 * math.exp(-0.3 * l)
        p = {
            'norm_mix_pre': norm_mix_pre[l], 'norm_mix_post': norm_mix_post[l],
            'norm_ffn_pre': norm_ffn_pre[l], 'norm_ffn_post': norm_ffn_post[l],
            'w_in': w_in[l], 'w_out': w_out[l],
            'lambda_q1': lambda_q1[l], 'lambda_k1': lambda_k1[l],
            'lambda_q2': lambda_q2[l], 'lambda_k2': lambda_k2[l], 'subln_w': subln_w[l],
            'rw_mu': rw_mu[l], 'rw_w0': rw_w0[l], 'rw_w2': rw_w2[l], 'rw_a0': rw_a0[l],
            'rw_a2': rw_a2[l], 'rw_g2': rw_g2[l], 'rw_kk': rw_kk[l], 'rw_ka': rw_ka[l],
            'rw_rk': rw_rk[l], 'rw_lnx_w': rw_lnx_w[l], 'rw_lnx_b': rw_lnx_b[l],
            'ffn_gate': ffn_gate[l], 'ffn_up': ffn_up[l], 'ffn_down': ffn_down[l],
        }
        B = yp.shape[0]
        shift0 = jnp.zeros((B, RW_PROJ_W), yp.dtype)
        wkv0 = jnp.zeros((B, RW_HEADS, RW_HEAD, RW_HEAD), yp.dtype)
        yp, kp, vp, wp, sp = _layer(yp, shift0, wkv0, _attend_prompt, p, lam_init)
        attend_s = functools.partial(_attend_paged, cache_k, cache_v, page_table, l)
        ys, ks, vs, ws, ss = _layer(ys, state_shift[l], state_wkv[l], attend_s, p, lam_init)
        kp_l.append(kp); vp_l.append(vp); wp_l.append(wp); sp_l.append(sp)
        ks_l.append(ks); vs_l.append(vs); ws_l.append(ws); ss_l.append(ss)
    k_prompt = jnp.stack(kp_l)
    v_prompt = jnp.stack(vp_l)
    wkv_prompt = jnp.stack(wp_l)
    shift_prompt = jnp.stack(sp_l)
    k_sample = jnp.stack(ks_l)
    v_sample = jnp.stack(vs_l)
    wkv_sample = jnp.stack(ws_l)
    shift_sample = jnp.stack(ss_l)
    return (yp, ys, k_prompt, v_prompt, wkv_prompt, shift_prompt, k_sample, v_sample, wkv_sample, shift_sample)
```

```python
import functools
import math

import jax
import jax.numpy as jnp
from jax import lax
from jax.experimental import pallas as pl
from jax.experimental.pallas import tpu as pltpu

F32 = jnp.float32
BF16 = jnp.bfloat16

D_MODEL = 1024
ATT_W = 512
RW_W = 512
ATT_HEADS = 4
ATT_QK_DIM = 64
ATT_V_DIM = 128
RW_HEAD = 64
RW_HEADS = 8
RW_PAIRS = RW_HEADS // 2
LORA_OFF = 3 * RW_W
GATE_OFF = LORA_OFF + 128
RW_PROJ_W = 1792
IN_W = 3 * ATT_W + RW_PROJ_W
D_FF = 2816
RMS_EPS = 1e-6
SUBLN_EPS = 1e-5
GN_EPS = 64e-5
NEG_BIG = -1e30

LANES = 128
VMEM_LIMIT = 56 * 1024 * 1024

WKV_CHUNK = 64
INV_BASE = 8


def _cparams(sem):
    return pltpu.CompilerParams(dimension_semantics=sem, vmem_limit_bytes=VMEM_LIMIT)


def _full(shape):
    return pl.BlockSpec(shape, lambda *_: (0,) * len(shape))


def _mm(a, b):
    return jnp.dot(a.astype(BF16), b.astype(BF16), preferred_element_type=F32)


def _mm_nt(a, b):
    return lax.dot_general(a.astype(BF16), b.astype(BF16), (((1,), (1,)), ((), ())),
                           preferred_element_type=F32)


def _split2(x):
    hi = x.astype(BF16)
    lo = (x - hi.astype(F32)).astype(BF16)
    return hi, lo


def _split3(x):
    hi = x.astype(BF16)
    r1 = x - hi.astype(F32)
    mid = r1.astype(BF16)
    lo = (r1 - mid.astype(F32)).astype(BF16)
    return hi, mid, lo


def _mm_exact_rhs(a, b_bf16):
    h1, h2, h3 = _split3(a)
    d = functools.partial(jnp.dot, preferred_element_type=F32)
    return d(h1, b_bf16) + d(h2, b_bf16) + d(h3, b_bf16)


def _mm_exact_lhs(a_bf16, b):
    h1, h2, h3 = _split3(b)
    d = functools.partial(jnp.dot, preferred_element_type=F32)
    return d(a_bf16, h1) + d(a_bf16, h2) + d(a_bf16, h3)


def _rms(x, g, eps):
    return x * lax.rsqrt(jnp.mean(x * x, axis=-1, keepdims=True) + eps) * g


def _diff_lambda(lq1, lk1, lq2, lk2, lam_init):
    s1 = jnp.sum(lq1[...] * lk1[...], axis=-1, keepdims=True)
    s2 = jnp.sum(lq2[...] * lk2[...], axis=-1, keepdims=True)
    return jnp.exp(s1) - jnp.exp(s2) + lam_init


def _inproj_kernel(x_ref, g_ref, w_ref, q_ref, k_ref, v_ref, rw_ref):
    h = _rms(x_ref[...], g_ref[...], RMS_EPS).astype(BF16)
    dot = functools.partial(jnp.dot, preferred_element_type=F32)
    q_ref[...] = dot(h, w_ref[:, 0:ATT_W])
    k_ref[...] = dot(h, w_ref[:, ATT_W:2 * ATT_W])
    v_ref[...] = dot(h, w_ref[:, 2 * ATT_W:3 * ATT_W])
    rw_ref[...] = dot(h, w_ref[:, 3 * ATT_W:])


def _inproj(x, g, w_in_bf16, tm):
    m = x.shape[0]
    row = lambda w: pl.BlockSpec((tm, w), lambda i: (i, 0))
    return pl.pallas_call(
        _inproj_kernel,
        grid=(m // tm,),
        in_specs=[row(D_MODEL), _full((1, D_MODEL)), _full((D_MODEL, IN_W))],
        out_specs=[row(ATT_W), row(ATT_W), row(ATT_W), row(RW_PROJ_W)],
        out_shape=[jax.ShapeDtypeStruct((m, ATT_W), F32)] * 3
        + [jax.ShapeDtypeStruct((m, RW_PROJ_W), F32)],
        compiler_params=_cparams(("parallel",)),
        name="inproj",
    )(x, g, w_in_bf16)


def _attn_prompt_kernel(lq1, lk1, lq2, lk2, sw_ref, q_ref, k_ref, v_ref, o_ref, *, tq, lam_init):
    i = pl.program_id(2)
    lam = _diff_lambda(lq1, lk1, lq2, lk2, lam_init)
    q = q_ref[...] * (ATT_QK_DIM ** -0.5)
    lane = lax.broadcasted_iota(jnp.int32, (tq, LANES), 1)
    qs = jnp.concatenate([jnp.where(lane < ATT_QK_DIM, q, 0.0),
                          jnp.where(lane >= ATT_QK_DIM, q, 0.0)], axis=0).astype(BF16)

    def update(j, carry, masked):
        m, l, acc = carry
        start = pl.multiple_of(j * tq, tq)
        kb = k_ref[pl.ds(start, tq), :].astype(BF16)
        vb = v_ref[pl.ds(start, tq), :].astype(BF16)
        s = _mm_nt(qs, kb)
        if masked:
            r = lax.broadcasted_iota(jnp.int32, (2 * tq, tq), 0)
            c = lax.broadcasted_iota(jnp.int32, (2 * tq, tq), 1)
            r = jnp.where(r >= tq, r - tq, r)
            s = jnp.where(r >= c, s, -jnp.inf)
        m_new = jnp.maximum(m, jnp.max(s, axis=-1, keepdims=True))
        alpha = jnp.exp(m - m_new)
        p = jnp.exp(s - m_new)
        l = alpha * l + jnp.sum(p, axis=-1, keepdims=True)
        acc = alpha * acc + jnp.dot(p.astype(BF16), vb, preferred_element_type=F32)
        return m_new, l, acc

    init = (jnp.full((2 * tq, 1), NEG_BIG, F32), jnp.zeros((2 * tq, 1), F32),
            jnp.zeros((2 * tq, LANES), F32))
    carry = lax.fori_loop(0, i, lambda j, c: update(j, c, False), init)
    m, l, acc = update(i, carry, True)
    o = acc / l
    o = o[:tq] - lam * o[tq:]
    o_ref[...] = _rms(o, sw_ref[...], SUBLN_EPS) * (1.0 - lam_init)


def _attn_prompt(q, k, v, lam_params, subln_w, batch, seq, lam_init, tq=256):
    nq = seq // tq
    lam_specs = [_full((1, ATT_QK_DIM))] * 4
    qspec = pl.BlockSpec((tq, LANES), lambda b, h, i: (b * nq + i, h))
    kvspec = pl.BlockSpec((seq, LANES), lambda b, h, i: (b, h))
    return pl.pallas_call(
        functools.partial(_attn_prompt_kernel, tq=tq, lam_init=lam_init),
        grid=(batch, ATT_HEADS, nq),
        in_specs=lam_specs + [_full((1, ATT_V_DIM)), qspec, kvspec, kvspec],
        out_specs=qspec,
        out_shape=jax.ShapeDtypeStruct((batch * seq, ATT_W), F32),
        compiler_params=_cparams(("parallel", "parallel", "arbitrary")),
        name="attn_prompt",
    )(*lam_params, subln_w, q, k, v)


def _attn_paged_kernel(pt_ref, lq1, lk1, lq2, lk2, sw_ref, q_ref, kn_ref, vn_ref, *rest,
                       pages_per_step, n_tok, lam_init):
    del pt_ref
    k_refs = rest[:pages_per_step]
    v_refs = rest[pages_per_step:2 * pages_per_step]
    o_ref, m_sc, l_sc, acc_sc = rest[2 * pages_per_step:]
    c = pl.program_id(1)
    half = n_tok * ATT_HEADS
    rows = 2 * half
    page_rows = k_refs[0].shape[1]

    @pl.when(c == 0)
    def _():
        m_sc[...] = jnp.full(m_sc.shape, NEG_BIG, F32)
        l_sc[...] = jnp.zeros(l_sc.shape, F32)
        acc_sc[...] = jnp.zeros(acc_sc.shape, F32)

    row = lax.broadcasted_iota(jnp.int32, (rows, LANES), 0)
    lane = lax.broadcasted_iota(jnp.int32, (rows, LANES), 1)
    qs = jnp.where((lane // ATT_QK_DIM) == (row // half), q_ref[0], 0.0) * (ATT_QK_DIM ** -0.5)
    qs_b = qs.astype(BF16)
    srow = lax.broadcasted_iota(jnp.int32, (rows, page_rows), 0)
    scol = lax.broadcasted_iota(jnp.int32, (rows, page_rows), 1)
    same_head = (scol % ATT_HEADS) == (srow % ATT_HEADS)

    m, l, acc = m_sc[...], l_sc[...], acc_sc[...]
    for kr, vr in zip(k_refs, v_refs):
        s = jnp.where(same_head, _mm_nt(qs_b, kr[0]), -jnp.inf)
        m_new = jnp.maximum(m, jnp.max(s, axis=-1, keepdims=True))
        alpha = jnp.exp(m - m_new)
        p = jnp.exp(s - m_new)
        l = alpha * l + jnp.sum(p, axis=-1, keepdims=True)
        acc = alpha * acc + _mm(p, vr[0])
        m = m_new
    m_sc[...], l_sc[...], acc_sc[...] = m, l, acc

    @pl.when(c == pl.num_programs(1) - 1)
    def _():
        lam = _diff_lambda(lq1, lk1, lq2, lk2, lam_init)
        mm, ll, aa = m, l, acc
        r1 = lax.broadcasted_iota(jnp.int32, (rows, 1), 0)
        tok = (r1 // ATT_HEADS) % n_tok
        for j in range(n_tok):
            s = jnp.sum(qs * kn_ref[0, j], axis=-1, keepdims=True)
            s = jnp.where(tok >= j, s, -jnp.inf)
            m_new = jnp.maximum(mm, s)
            alpha = jnp.exp(mm - m_new)
            p = jnp.exp(s - m_new)
            ll = alpha * ll + p
            aa = alpha * aa + p * vn_ref[0, j]
            mm = m_new
        z = aa / ll
        o = z[:half] - lam * z[half:]
        o_ref[0] = _rms(o, sw_ref[...], SUBLN_EPS) * (1.0 - lam_init)


def _attn_paged(q, kn, vn, cache_k, cache_v, page_table, lam_params, subln_w, lam_init,
                pages_per_step=8):
    db, n_tok, _ = q.shape
    n_pages = page_table.shape[1]
    page_rows = cache_k.shape[1]
    steps = n_pages // pages_per_step
    half = n_tok * ATT_HEADS
    rows = 2 * half
    pt_flat = page_table.reshape(-1)
    heads = lambda a: a.reshape(db, n_tok, ATT_HEADS, LANES)
    q_rows = jnp.broadcast_to(heads(q)[:, None], (db, 2, n_tok, ATT_HEADS, LANES)).reshape(db, rows, LANES)
    rep = lambda a: jnp.broadcast_to(
        heads(a)[:, :, None, None], (db, n_tok, 2, n_tok, ATT_HEADS, LANES)).reshape(db, n_tok, rows, LANES)

    def page_spec(i):
        return pl.BlockSpec(
            (1, page_rows, LANES),
            lambda b, c, pt: (pt[b * n_pages + c * pages_per_step + i], 0, 0))

    const = lambda shape: pl.BlockSpec(shape, lambda b, c, pt: (0,) * len(shape))
    new_spec = pl.BlockSpec((1, n_tok, rows, LANES), lambda b, c, pt: (b, 0, 0, 0))
    grid_spec = pltpu.PrefetchScalarGridSpec(
        num_scalar_prefetch=1,
        grid=(db, steps),
        in_specs=[const((1, ATT_QK_DIM))] * 4 + [const((1, ATT_V_DIM))]
        + [pl.BlockSpec((1, rows, LANES), lambda b, c, pt: (b, 0, 0)), new_spec, new_spec]
        + [page_spec(i) for i in range(pages_per_step)] * 2,
        out_specs=pl.BlockSpec((1, half, LANES), lambda b, c, pt: (b, 0, 0)),
        scratch_shapes=[pltpu.VMEM((rows, 1), F32), pltpu.VMEM((rows, 1), F32),
                        pltpu.VMEM((rows, LANES), F32)],
    )
    out = pl.pallas_call(
        functools.partial(_attn_paged_kernel, pages_per_step=pages_per_step, n_tok=n_tok,
                          lam_init=lam_init),
        grid_spec=grid_spec,
        out_shape=jax.ShapeDtypeStruct((db, half, LANES), F32),
        compiler_params=_cparams(("parallel", "arbitrary")),
        name="attn_paged",
    )(pt_flat, *lam_params, subln_w, q_rows, rep(kn), rep(vn),
      *([cache_k] * pages_per_step), *([cache_v] * pages_per_step))
    return out.reshape(db, n_tok, ATT_W)


def _softplus(x):
    return jnp.maximum(x, 0.0) + jnp.log1p(jnp.exp(-jnp.abs(x)))


def _sigmoid(x):
    return 1.0 / (1.0 + jnp.exp(-x))


def _rwkv_pre_kernel(rw_ref, first_ref, mu_ref, w0_ref, a0_ref, kkw_ref, ka_ref, rk_ref,
                     w2_ref, a2_ref, g2_ref, seg_ref,
                     r_ref, lw_ref, k_ref, v_ref, kap_ref, b_ref, g_ref, bonus_ref,
                     *, period):
    x = rw_ref[...]
    tm = x.shape[0]
    row = lax.broadcasted_iota(jnp.int32, x.shape, 0)
    prev = pltpu.roll(x, 1, axis=0)
    prev = jnp.where(row % period == 0, first_ref[0], prev)
    xs = x + (prev - x) * mu_ref[...]
    r = xs[:, 0:RW_W]
    k = xs[:, RW_W:2 * RW_W]
    v = xs[:, 2 * RW_W:3 * RW_W]
    lo = xs[:, LORA_OFF:LORA_OFF + 128]
    g_lo = xs[:, GATE_OFF:GATE_OFF + 128]
    w_raw = -_softplus(-(w0_ref[...] + _mm(jnp.tanh(lo), w2_ref[...]))) - 0.5
    lw_ref[...] = -jnp.exp(w_raw)
    a = _sigmoid(a0_ref[...] + _mm(lo, a2_ref[...]))
    g_ref[...] = _mm(_sigmoid(g_lo), g2_ref[...])
    kk = k * kkw_ref[...]
    norm = jnp.sqrt(_mm_exact_rhs(kk * kk, seg_ref[...]))
    kap = kk / jnp.maximum(norm, 1e-12)
    k2 = k * (1.0 + (a - 1.0) * ka_ref[...])
    r_ref[...] = r
    k_ref[...] = k2
    v_ref[...] = v
    kap_ref[...] = kap
    b_ref[...] = kap * a
    bonus_ref[...] = _mm_exact_rhs(r * k2 * rk_ref[...], seg_ref[...]) * v


def _rwkv_pre(rw, first, period, p, tm):
    m = rw.shape[0]
    row = lambda w: pl.BlockSpec((tm, w), lambda i: (i, 0))
    vec = _full((1, RW_W))
    out = jax.ShapeDtypeStruct((m, RW_W), F32)
    first_spec = pl.BlockSpec((1,) + first.shape[1:], lambda i: (i, 0, 0))
    return pl.pallas_call(
        functools.partial(_rwkv_pre_kernel, period=period),
        grid=(m // tm,),
        in_specs=[row(RW_PROJ_W), first_spec, _full((1, RW_PROJ_W)), vec, vec, vec, vec, vec,
                  _full((128, RW_W)), _full((128, RW_W)), _full((128, RW_W)), _full((RW_W, RW_W))],
        out_specs=[row(RW_W)] * 8,
        out_shape=[out] * 8,
        compiler_params=_cparams(("parallel",)),
        name="rwkv_pre",
    )(rw, first, p['mu'], p['w0'], p['a0'], p['kk'], p['ka'], p['rk'],
      p['w2pad'], p['a2pad'], p['g2'], p['seg'])


def _stack(x):
    lane = lax.broadcasted_iota(jnp.int32, x.shape, 1)
    return jnp.concatenate([jnp.where(lane < RW_HEAD, x, 0.0),
                            jnp.where(lane >= RW_HEAD, x, 0.0)], axis=0)


def _unit_lower_inverse(lmat, c):
    n = 2 * c
    ri = lax.broadcasted_iota(jnp.int32, (n, n), 0)
    ci = lax.broadcasted_iota(jnp.int32, (n, n), 1)
    eye = jnp.where(ri == ci, 1.0, 0.0)
    same = lambda s: (ri // s) == (ci // s)
    m1 = jnp.where(same(INV_BASE), -lmat, 0.0)
    m2 = _mm(m1, m1)
    m4 = _mm(m2, m2)
    t = eye + m1 + m2 + _mm(m1, m2)
    t = t + _mm(t, m4)
    s = INV_BASE
    while s < c:
        off = jnp.where(same(2 * s) & jnp.logical_not(same(s)), lmat, 0.0)
        t = t - _mm(t, _mm(off, t))
        s *= 2
    return t


def _wkv_scan_kernel(r_ref, lw_ref, k_ref, v_ref, kap_ref, b_ref, s0_ref, tri_ref,
                     y_ref, sout_ref, st_sc, *, c):
    ci = pl.program_id(1)
    n = 2 * c

    @pl.when(ci == 0)
    def _():
        zero = jnp.zeros((RW_HEAD, RW_HEAD), F32)
        for p in range(RW_PAIRS):
            top = jnp.concatenate([s0_ref[0, 2 * p], zero], axis=1)
            bot = jnp.concatenate([zero, s0_ref[0, 2 * p + 1]], axis=1)
            st_sc[p] = jnp.concatenate([top, bot], axis=0)

    ri = lax.broadcasted_iota(jnp.int32, (n, n), 0)
    cj = lax.broadcasted_iota(jnp.int32, (n, n), 1)
    same_head = (ri // c) == (cj // c)
    strict = same_head & (ri > cj)
    incl = same_head & (ri >= cj)
    tri = tri_ref[...]

    for p in range(RW_PAIRS):
        sl = slice(p * LANES, (p + 1) * LANES)
        lw = lw_ref[:, sl]
        cs = _mm_exact_lhs(tri, lw)
        cs_end = cs[c - 1:c, :]
        e_in = jnp.exp(cs)
        e_out = jnp.exp(-cs)
        kh = kap_ref[:, sl] * jnp.exp(cs - lw)
        rt = r_ref[:, sl] * e_in
        kt = k_ref[:, sl] * e_out
        bt = b_ref[:, sl] * e_out
        e_end = jnp.exp(cs_end - cs)
        k_end = k_ref[:, sl] * e_end
        b_end = b_ref[:, sl] * e_end
        v = v_ref[:, sl]

        khs, rts = _stack(kh), _stack(rt)
        bb = jnp.concatenate([bt, bt], axis=0)
        kk = jnp.concatenate([kt, kt], axis=0)
        lmat = jnp.where(strict, _mm_nt(khs, bb), 0.0)
        a_kk = jnp.where(strict, _mm_nt(khs, kk), 0.0)
        a_rb = jnp.where(incl, _mm_nt(rts, bb), 0.0)
        a_rk = jnp.where(incl, _mm_nt(rts, kk), 0.0)
        tinv = _unit_lower_inverse(lmat, c)

        st = st_sc[p]
        vs = _stack(v)
        u = _mm(tinv, _stack(_mm_nt(kh, st)) + _mm(a_kk, vs))
        ys = _stack(_mm_nt(rt, st)) + _mm(a_rk, vs) - _mm(a_rb, u)
        y_ref[:, sl] = ys[:c] + ys[c:]
        vu = jnp.concatenate([vs, u], axis=0)
        kb = jnp.concatenate([_stack(k_end), -_stack(b_end)], axis=0)
        st_new = st * jnp.exp(cs_end) + _mm(vu.T, kb)
        st_sc[p] = st_new

        @pl.when(ci == pl.num_programs(1) - 1)
        def _():
            sout_ref[0, 2 * p] = st_new[:RW_HEAD, :RW_HEAD]
            sout_ref[0, 2 * p + 1] = st_new[RW_HEAD:, RW_HEAD:]


def _wkv_scan(r, lw, k, v, kap, b, s0, batch, seq, c):
    nc = seq // c
    tile = pl.BlockSpec((c, RW_W), lambda bi, ci: (bi * nc + ci, 0))
    sspec = pl.BlockSpec((1, RW_HEADS, RW_HEAD, RW_HEAD), lambda bi, ci: (bi, 0, 0, 0))
    tri = jnp.tril(jnp.ones((c, c), F32)).astype(BF16)
    return pl.pallas_call(
        functools.partial(_wkv_scan_kernel, c=c),
        grid=(batch, nc),
        in_specs=[tile] * 6 + [sspec, _full((c, c))],
        out_specs=[tile, sspec],
        out_shape=[jax.ShapeDtypeStruct((batch * seq, RW_W), F32),
                   jax.ShapeDtypeStruct((batch, RW_HEADS, RW_HEAD, RW_HEAD), F32)],
        scratch_shapes=[pltpu.VMEM((RW_PAIRS, LANES, LANES), F32)],
        compiler_params=_cparams(("parallel", "arbitrary")),
        name="wkv_scan",
    )(r, lw, k, v, kap, b, s0, tri)


def _mix_out_kernel(x_ref, att_ref, y_ref, bonus_ref, g_ref, lnw_ref, lnb_ref, seg_ref,
                    woa_ref, wor_ref, gpost_ref, o_ref):
    y = y_ref[...]
    seg = seg_ref[...]
    mean = _mm_exact_rhs(y, seg) * (1.0 / RW_HEAD)
    d = y - mean
    var = _mm_exact_rhs(d * d, seg) * (1.0 / RW_HEAD)
    yn = d * lax.rsqrt(var + GN_EPS) * lnw_ref[...] + lnb_ref[...]
    rw_out = (yn + bonus_ref[...]) * g_ref[...]
    mix = _mm(att_ref[...], woa_ref[...]) + _mm(rw_out, wor_ref[...])
    o_ref[...] = x_ref[...] + _rms(mix, gpost_ref[...], RMS_EPS)


def _mix_out(x, att, y, bonus, g, p, tm):
    m = x.shape[0]
    row = lambda w: pl.BlockSpec((tm, w), lambda i: (i, 0))
    vec = _full((1, RW_W))
    return pl.pallas_call(
        _mix_out_kernel,
        grid=(m // tm,),
        in_specs=[row(D_MODEL), row(ATT_W), row(RW_W), row(RW_W), row(RW_W), vec, vec,
                  _full((RW_W, RW_W)), _full((ATT_W, D_MODEL)), _full((RW_W, D_MODEL)),
                  _full((1, D_MODEL))],
        out_specs=row(D_MODEL),
        out_shape=jax.ShapeDtypeStruct((m, D_MODEL), F32),
        compiler_params=_cparams(("parallel",)),
        name="mix_out",
    )(x, att, y, bonus, g, p['lnx_w'], p['lnx_b'], p['seg'], p['w_out_att'], p['w_out_rw'],
      p['norm_mix_post'])


def _ffn_kernel(x_ref, gpre_ref, gpost_ref, wg_ref, wu_ref, wd_ref, o_ref, f_sc, acc_sc):
    j = pl.program_id(1)

    @pl.when(j == 0)
    def _():
        f_sc[...] = _rms(x_ref[...], gpre_ref[...], RMS_EPS).astype(BF16)
        acc_sc[...] = jnp.zeros(acc_sc.shape, F32)

    f = f_sc[...]
    gate = jnp.dot(f, wg_ref[...], preferred_element_type=F32)
    up = jnp.dot(f, wu_ref[...], preferred_element_type=F32)
    h = gate * _sigmoid(gate) * up
    acc_sc[...] += jnp.dot(h.astype(BF16), wd_ref[...], preferred_element_type=F32)

    @pl.when(j == pl.num_programs(1) - 1)
    def _():
        o_ref[...] = x_ref[...] + _rms(acc_sc[...], gpost_ref[...], RMS_EPS)


def _ffn(x, p, tm, tff):
    m = x.shape[0]
    row = pl.BlockSpec((tm, D_MODEL), lambda i, j: (i, 0))
    vec = pl.BlockSpec((1, D_MODEL), lambda i, j: (0, 0))
    return pl.pallas_call(
        _ffn_kernel,
        grid=(m // tm, D_FF // tff),
        in_specs=[row, vec, vec,
                  pl.BlockSpec((D_MODEL, tff), lambda i, j: (0, j)),
                  pl.BlockSpec((D_MODEL, tff), lambda i, j: (0, j)),
                  pl.BlockSpec((tff, D_MODEL), lambda i, j: (j, 0))],
        out_specs=row,
        out_shape=jax.ShapeDtypeStruct((m, D_MODEL), F32),
        scratch_shapes=[pltpu.VMEM((tm, D_MODEL), BF16), pltpu.VMEM((tm, D_MODEL), F32)],
        compiler_params=_cparams(("parallel", "arbitrary")),
        name="ffn",
    )(x, p['norm_ffn_pre'], p['norm_ffn_post'], p['ffn_gate'], p['ffn_up'], p['ffn_down'])


def _layer_params(l, w):
    row = lambda a: a[l].reshape(1, -1).astype(F32)
    zeros64 = jnp.zeros((64, RW_W), F32)
    head = jnp.arange(RW_W) // RW_HEAD
    return {
        'norm_mix_pre': row(w['norm_mix_pre']), 'norm_mix_post': row(w['norm_mix_post']),
        'norm_ffn_pre': row(w['norm_ffn_pre']), 'norm_ffn_post': row(w['norm_ffn_post']),
        'w_in': w['w_in'][l].astype(BF16),
        'w_out_att': w['w_out'][l][:ATT_W].astype(BF16),
        'w_out_rw': w['w_out'][l][ATT_W:].astype(BF16),
        'lam': [row(w[n]) for n in ('lambda_q1', 'lambda_k1', 'lambda_q2', 'lambda_k2')],
        'subln_w': row(w['subln_w']),
        'mu': row(w['rw_mu']), 'w0': row(w['rw_w0']), 'a0': row(w['rw_a0']),
        'kk': row(w['rw_kk']), 'ka': row(w['rw_ka']), 'rk': row(w['rw_rk']),
        'lnx_w': row(w['rw_lnx_w']), 'lnx_b': row(w['rw_lnx_b']),
        'w2pad': jnp.concatenate([w['rw_w2'][l], zeros64], axis=0).astype(BF16),
        'a2pad': jnp.concatenate([zeros64, w['rw_a2'][l]], axis=0).astype(BF16),
        'g2': w['rw_g2'][l].astype(BF16),
        'seg': (head[:, None] == head[None, :]).astype(BF16),
        'ffn_gate': w['ffn_gate'][l].astype(BF16), 'ffn_up': w['ffn_up'][l].astype(BF16),
        'ffn_down': w['ffn_down'][l].astype(BF16),
    }


def _rwkv_group(rw, shift0, wkv0, p, batch, seq):
    m = batch * seq
    if seq >= WKV_CHUNK:
        tm = min(512, seq)
        tiles = seq // tm
        last_rows = rw.reshape(batch, tiles, tm, RW_PROJ_W)[:, :, -1, :]
        lead = jnp.concatenate([shift0[:, None, :], last_rows[:, :-1, :]], axis=1)
        first = lead.reshape(m // tm, 1, RW_PROJ_W)
        period, c, pad = tm, WKV_CHUNK, 0
    else:
        tm = m
        first = jnp.zeros((batch, seq, RW_PROJ_W), F32).at[:, 0, :].set(shift0).reshape(1, m, RW_PROJ_W)
        period, c = seq, 8
        pad = c - seq
    r, lw, k, v, kap, b, g, bonus = _rwkv_pre(rw, first, period, p, tm)
    scan_in = (r, lw, k, v, kap, b)
    if pad:
        scan_in = [jnp.pad(t.reshape(batch, seq, RW_W), ((0, 0), (0, pad), (0, 0))).reshape(batch * c, RW_W)
                   for t in scan_in]
        y, s_new = _wkv_scan(*scan_in, wkv0, batch, c, c)
        y = y.reshape(batch, c, RW_W)[:, :seq].reshape(m, RW_W)
    else:
        y, s_new = _wkv_scan(*scan_in, wkv0, batch, seq, c)
    return y, bonus, g, s_new


def _layer(x, shift0, wkv0, attend, p, batch, seq):
    m = batch * seq
    tm = min(512, m)
    q, k, v, rw = _inproj(x, p['norm_mix_pre'], p['w_in'], tm)
    att = attend(q, k, v)
    y, bonus, g, wkv_new = _rwkv_group(rw, shift0, wkv0, p, batch, seq)
    x1 = _mix_out(x, att, y, bonus, g, p, tm)
    out = _ffn(x1, p, tm, D_FF // 2)
    shift_new = rw.reshape(batch, seq, RW_PROJ_W)[:, -1]
    return out, k, v, wkv_new, shift_new


def kernel(x_prompt, x_sample, cache_k, cache_v, state_wkv, state_shift, page_table, norm_mix_pre, norm_mix_post, norm_ffn_pre, norm_ffn_post, w_in, w_out, lambda_q1, lambda_k1, lambda_q2, lambda_k2, subln_w, rw_mu, rw_w0, rw_w2, rw_a0, rw_a2, rw_g2, rw_kk, rw_ka, rw_rk, rw_lnx_w, rw_lnx_b, ffn_gate, ffn_up, ffn_down):
    weights = dict(
        norm_mix_pre=norm_mix_pre, norm_mix_post=norm_mix_post, norm_ffn_pre=norm_ffn_pre,
        norm_ffn_post=norm_ffn_post, w_in=w_in, w_out=w_out, lambda_q1=lambda_q1,
        lambda_k1=lambda_k1, lambda_q2=lambda_q2, lambda_k2=lambda_k2, subln_w=subln_w,
        rw_mu=rw_mu, rw_w0=rw_w0, rw_w2=rw_w2, rw_a0=rw_a0, rw_a2=rw_a2, rw_g2=rw_g2,
        rw_kk=rw_kk, rw_ka=rw_ka, rw_rk=rw_rk, rw_lnx_w=rw_lnx_w, rw_lnx_b=rw_lnx_b,
        ffn_gate=ffn_gate, ffn_up=ffn_up, ffn_down=ffn_down)
    depth = w_in.shape[0]
    batch, seq, _ = x_prompt.shape
    db, dseq, _ = x_sample.shape
    n_pool, page = cache_k.shape[1], cache_k.shape[2]
    yp = x_prompt.reshape(batch * seq, D_MODEL)
    ys = x_sample.reshape(db * dseq, D_MODEL)
    outs = [[] for _ in range(8)]
    for l in range(depth):
        lam_init = 0.8 - 0.6 * math.exp(-0.3 * l)
        p = _layer_params(l, weights)
        ck = cache_k[l].reshape(n_pool, page * ATT_HEADS, LANES)
        cv = cache_v[l].reshape(n_pool, page * ATT_HEADS, LANES)

        def attend_p(q, k, v):
            return _attn_prompt(q, k, v, p['lam'], p['subln_w'], batch, seq, lam_init)

        def attend_s(q, k, v):
            t3 = lambda a: a.reshape(db, dseq, ATT_W)
            o = _attn_paged(t3(q), t3(k), t3(v), ck, cv, page_table, p['lam'], p['subln_w'], lam_init)
            return o.reshape(db * dseq, ATT_W)

        zshift = jnp.zeros((batch, RW_PROJ_W), F32)
        zwkv = jnp.zeros((batch, RW_HEADS, RW_HEAD, RW_HEAD), F32)
        yp, kp, vp, wp, sp = _layer(yp, zshift, zwkv, attend_p, p, batch, seq)
        ys, ks, vs, ws, ss = _layer(ys, state_shift[l], state_wkv[l], attend_s, p, db, dseq)
        kv_p = lambda a: a.reshape(batch, seq, ATT_HEADS, ATT_V_DIM)
        kv_s = lambda a: a.reshape(db, dseq, ATT_HEADS, ATT_V_DIM)
        for lst, val in zip(outs, (kv_p(kp), kv_p(vp), wp, sp, kv_s(ks), kv_s(vs), ws, ss)):
            lst.append(val)
    stacked = [jnp.stack(o) for o in outs]
    return (yp.reshape(batch, seq, D_MODEL), ys.reshape(db, dseq, D_MODEL), *stacked)
```

```python
import functools
import math

import jax
import jax.numpy as jnp
from jax import lax
from jax.experimental import pallas as pl
from jax.experimental.pallas import tpu as pltpu

F32 = jnp.float32
BF16 = jnp.bfloat16

D_MODEL = 1024
ATT_W = 512
RW_W = 512
ATT_HEADS = 4
ATT_QK_DIM = 64
ATT_V_DIM = 128
RW_HEAD = 64
RW_HEADS = 8
RW_PAIRS = RW_HEADS // 2
LORA_OFF = 3 * RW_W
GATE_OFF = LORA_OFF + 128
RW_PROJ_W = 1792
IN_W = 3 * ATT_W + RW_PROJ_W
D_FF = 2816
RMS_EPS = 1e-6
SUBLN_EPS = 1e-5
GN_EPS = 64e-5
NEG_BIG = -1e30

LANES = 128
VMEM_LIMIT = 56 * 1024 * 1024

WKV_CHUNK = 64
INV_BASE = 8


def _cparams(sem):
    return pltpu.CompilerParams(dimension_semantics=sem, vmem_limit_bytes=VMEM_LIMIT)


def _full(shape):
    return pl.BlockSpec(shape, lambda *_: (0,) * len(shape))


def _mm(a, b):
    return jnp.dot(a.astype(BF16), b.astype(BF16), preferred_element_type=F32)


def _mm_nt(a, b):
    return lax.dot_general(a.astype(BF16), b.astype(BF16), (((1,), (1,)), ((), ())),
                           preferred_element_type=F32)


def _split2(x):
    hi = x.astype(BF16)
    lo = (x - hi.astype(F32)).astype(BF16)
    return hi, lo


def _split3(x):
    hi = x.astype(BF16)
    r1 = x - hi.astype(F32)
    mid = r1.astype(BF16)
    lo = (r1 - mid.astype(F32)).astype(BF16)
    return hi, mid, lo


def _mm_exact_rhs(a, b_bf16):
    h1, h2, h3 = _split3(a)
    d = functools.partial(jnp.dot, preferred_element_type=F32)
    return d(h1, b_bf16) + d(h2, b_bf16) + d(h3, b_bf16)


def _mm_exact_lhs(a_bf16, b):
    h1, h2, h3 = _split3(b)
    d = functools.partial(jnp.dot, preferred_element_type=F32)
    return d(a_bf16, h1) + d(a_bf16, h2) + d(a_bf16, h3)


def _rms(x, g, eps):
    return x * lax.rsqrt(jnp.mean(x * x, axis=-1, keepdims=True) + eps) * g


def _diff_lambda(lq1, lk1, lq2, lk2, lam_init):
    s1 = jnp.sum(lq1[...] * lk1[...], axis=-1, keepdims=True)
    s2 = jnp.sum(lq2[...] * lk2[...], axis=-1, keepdims=True)
    return jnp.exp(s1) - jnp.exp(s2) + lam_init


def _inproj_kernel(x_ref, g_ref, w_ref, q_ref, k_ref, v_ref, rw_ref):
    h = _rms(x_ref[...], g_ref[...], RMS_EPS).astype(BF16)
    dot = functools.partial(jnp.dot, preferred_element_type=F32)
    q_ref[...] = dot(h, w_ref[:, 0:ATT_W])
    k_ref[...] = dot(h, w_ref[:, ATT_W:2 * ATT_W])
    v_ref[...] = dot(h, w_ref[:, 2 * ATT_W:3 * ATT_W])
    rw_ref[...] = dot(h, w_ref[:, 3 * ATT_W:])


def _inproj(x, g, w_in_bf16, tm):
    m = x.shape[0]
    row = lambda w: pl.BlockSpec((tm, w), lambda i: (i, 0))
    return pl.pallas_call(
        _inproj_kernel,
        grid=(m // tm,),
        in_specs=[row(D_MODEL), _full((1, D_MODEL)), _full((D_MODEL, IN_W))],
        out_specs=[row(ATT_W), row(ATT_W), row(ATT_W), row(RW_PROJ_W)],
        out_shape=[jax.ShapeDtypeStruct((m, ATT_W), F32)] * 3
        + [jax.ShapeDtypeStruct((m, RW_PROJ_W), F32)],
        compiler_params=_cparams(("parallel",)),
        name="inproj",
    )(x, g, w_in_bf16)


def _attn_prompt_kernel(lq1, lk1, lq2, lk2, sw_ref, q_ref, k_ref, v_ref, o_ref, *, tq, lam_init):
    i = pl.program_id(2)
    lam = _diff_lambda(lq1, lk1, lq2, lk2, lam_init)
    q = q_ref[...] * (ATT_QK_DIM ** -0.5)
    lane = lax.broadcasted_iota(jnp.int32, (tq, LANES), 1)
    qs = jnp.concatenate([jnp.where(lane < ATT_QK_DIM, q, 0.0),
                          jnp.where(lane >= ATT_QK_DIM, q, 0.0)], axis=0).astype(BF16)

    def update(j, carry, masked):
        m, l, acc = carry
        start = pl.multiple_of(j * tq, tq)
        kb = k_ref[pl.ds(start, tq), :].astype(BF16)
        vb = v_ref[pl.ds(start, tq), :].astype(BF16)
        s = _mm_nt(qs, kb)
        if masked:
            r = lax.broadcasted_iota(jnp.int32, (2 * tq, tq), 0)
            c = lax.broadcasted_iota(jnp.int32, (2 * tq, tq), 1)
            r = jnp.where(r >= tq, r - tq, r)
            s = jnp.where(r >= c, s, -jnp.inf)
        m_new = jnp.maximum(m, jnp.max(s, axis=-1, keepdims=True))
        alpha = jnp.exp(m - m_new)
        p = jnp.exp(s - m_new)
        l = alpha * l + jnp.sum(p, axis=-1, keepdims=True)
        acc = alpha * acc + jnp.dot(p.astype(BF16), vb, preferred_element_type=F32)
        return m_new, l, acc

    init = (jnp.full((2 * tq, 1), NEG_BIG, F32), jnp.zeros((2 * tq, 1), F32),
            jnp.zeros((2 * tq, LANES), F32))
    carry = lax.fori_loop(0, i, lambda j, c: update(j, c, False), init)
    m, l, acc = update(i, carry, True)
    o = acc / l
    o = o[:tq] - lam * o[tq:]
    o_ref[...] = _rms(o, sw_ref[...], SUBLN_EPS) * (1.0 - lam_init)


def _attn_prompt(q, k, v, lam_params, subln_w, batch, seq, lam_init, tq=256):
    nq = seq // tq
    lam_specs = [_full((1, ATT_QK_DIM))] * 4
    qspec = pl.BlockSpec((tq, LANES), lambda b, h, i: (b * nq + i, h))
    kvspec = pl.BlockSpec((seq, LANES), lambda b, h, i: (b, h))
    return pl.pallas_call(
        functools.partial(_attn_prompt_kernel, tq=tq, lam_init=lam_init),
        grid=(batch, ATT_HEADS, nq),
        in_specs=lam_specs + [_full((1, ATT_V_DIM)), qspec, kvspec, kvspec],
        out_specs=qspec,
        out_shape=jax.ShapeDtypeStruct((batch * seq, ATT_W), F32),
        compiler_params=_cparams(("parallel", "parallel", "arbitrary")),
        name="attn_prompt",
    )(*lam_params, subln_w, q, k, v)


def _attn_paged_kernel(pt_ref, lq1, lk1, lq2, lk2, sw_ref, q_ref, kn_ref, vn_ref, *rest,
                       pages_per_step, n_tok, lam_init):
    del pt_ref
    k_refs = rest[:pages_per_step]
    v_refs = rest[pages_per_step:2 * pages_per_step]
    o_ref, m_sc, l_sc, acc_sc = rest[2 * pages_per_step:]
    c = pl.program_id(1)
    half = n_tok * ATT_HEADS
    rows = 2 * half
    page_rows = k_refs[0].shape[1]

    @pl.when(c == 0)
    def _():
        m_sc[...] = jnp.full(m_sc.shape, NEG_BIG, F32)
        l_sc[...] = jnp.zeros(l_sc.shape, F32)
        acc_sc[...] = jnp.zeros(acc_sc.shape, F32)

    row = lax.broadcasted_iota(jnp.int32, (rows, LANES), 0)
    lane = lax.broadcasted_iota(jnp.int32, (rows, LANES), 1)
    qs = jnp.where((lane // ATT_QK_DIM) == (row // half), q_ref[0], 0.0) * (ATT_QK_DIM ** -0.5)
    qs_b = qs.astype(BF16)
    srow = lax.broadcasted_iota(jnp.int32, (rows, page_rows), 0)
    scol = lax.broadcasted_iota(jnp.int32, (rows, page_rows), 1)
    same_head = (scol % ATT_HEADS) == (srow % ATT_HEADS)

    ss = [jnp.where(same_head, _mm_nt(qs_b, kr[0]), -jnp.inf) for kr in k_refs]
    m_old, l_old, acc_old = m_sc[...], l_sc[...], acc_sc[...]
    m = jnp.maximum(m_old, jnp.max(functools.reduce(jnp.maximum, ss), axis=-1, keepdims=True))
    alpha = jnp.exp(m_old - m)
    ps = [jnp.exp(s - m) for s in ss]
    l = alpha * l_old + jnp.sum(functools.reduce(jnp.add, ps), axis=-1, keepdims=True)
    pv = functools.reduce(jnp.add, [_mm(p, vr[0]) for p, vr in zip(ps, v_refs)])
    acc = alpha * acc_old + pv
    m_sc[...], l_sc[...], acc_sc[...] = m, l, acc

    @pl.when(c == pl.num_programs(1) - 1)
    def _():
        lam = _diff_lambda(lq1, lk1, lq2, lk2, lam_init)
        mm, ll, aa = m, l, acc
        r1 = lax.broadcasted_iota(jnp.int32, (rows, 1), 0)
        tok = (r1 // ATT_HEADS) % n_tok
        for j in range(n_tok):
            s = jnp.sum(qs * kn_ref[0, j], axis=-1, keepdims=True)
            s = jnp.where(tok >= j, s, -jnp.inf)
            m_new = jnp.maximum(mm, s)
            alpha = jnp.exp(mm - m_new)
            p = jnp.exp(s - m_new)
            ll = alpha * ll + p
            aa = alpha * aa + p * vn_ref[0, j]
            mm = m_new
        z = aa / ll
        o = z[:half] - lam * z[half:]
        o_ref[0] = _rms(o, sw_ref[...], SUBLN_EPS) * (1.0 - lam_init)


def _attn_paged(q, kn, vn, cache_k, cache_v, page_table, lam_params, subln_w, lam_init,
                pages_per_step=16):
    db, n_tok, _ = q.shape
    n_pages = page_table.shape[1]
    page_rows = cache_k.shape[1]
    pages_per_step = math.gcd(pages_per_step, n_pages)
    steps = n_pages // pages_per_step
    half = n_tok * ATT_HEADS
    rows = 2 * half
    pt_flat = page_table.reshape(-1)
    heads = lambda a: a.reshape(db, n_tok, ATT_HEADS, LANES)
    q_rows = jnp.broadcast_to(heads(q)[:, None], (db, 2, n_tok, ATT_HEADS, LANES)).reshape(db, rows, LANES)
    rep = lambda a: jnp.broadcast_to(
        heads(a)[:, :, None, None], (db, n_tok, 2, n_tok, ATT_HEADS, LANES)).reshape(db, n_tok, rows, LANES)

    def page_spec(i):
        return pl.BlockSpec(
            (1, page_rows, LANES),
            lambda b, c, pt: (pt[b * n_pages + c * pages_per_step + i], 0, 0))

    const = lambda shape: pl.BlockSpec(shape, lambda b, c, pt: (0,) * len(shape))
    new_spec = pl.BlockSpec((1, n_tok, rows, LANES), lambda b, c, pt: (b, 0, 0, 0))
    grid_spec = pltpu.PrefetchScalarGridSpec(
        num_scalar_prefetch=1,
        grid=(db, steps),
        in_specs=[const((1, ATT_QK_DIM))] * 4 + [const((1, ATT_V_DIM))]
        + [pl.BlockSpec((1, rows, LANES), lambda b, c, pt: (b, 0, 0)), new_spec, new_spec]
        + [page_spec(i) for i in range(pages_per_step)] * 2,
        out_specs=pl.BlockSpec((1, half, LANES), lambda b, c, pt: (b, 0, 0)),
        scratch_shapes=[pltpu.VMEM((rows, 1), F32), pltpu.VMEM((rows, 1), F32),
                        pltpu.VMEM((rows, LANES), F32)],
    )
    out = pl.pallas_call(
        functools.partial(_attn_paged_kernel, pages_per_step=pages_per_step, n_tok=n_tok,
                          lam_init=lam_init),
        grid_spec=grid_spec,
        out_shape=jax.ShapeDtypeStruct((db, half, LANES), F32),
        compiler_params=_cparams(("parallel", "arbitrary")),
        name="attn_paged",
    )(pt_flat, *lam_params, subln_w, q_rows, rep(kn), rep(vn),
      *([cache_k] * pages_per_step), *([cache_v] * pages_per_step))
    return out.reshape(db, n_tok, ATT_W)


def _softplus(x):
    return jnp.maximum(x, 0.0) + jnp.log1p(jnp.exp(-jnp.abs(x)))


def _sigmoid(x):
    return 1.0 / (1.0 + jnp.exp(-x))


def _rwkv_pre_kernel(rw_ref, first_ref, mu_ref, w0_ref, a0_ref, kkw_ref, ka_ref, rk_ref,
                     w2_ref, a2_ref, g2_ref, seg_ref,
                     r_ref, lw_ref, k_ref, v_ref, kap_ref, b_ref, g_ref, bonus_ref,
                     *, period):
    x = rw_ref[...]
    tm = x.shape[0]
    row = lax.broadcasted_iota(jnp.int32, x.shape, 0)
    prev = pltpu.roll(x, 1, axis=0)
    prev = jnp.where(row % period == 0, first_ref[0], prev)
    xs = x + (prev - x) * mu_ref[...]
    r = xs[:, 0:RW_W]
    k = xs[:, RW_W:2 * RW_W]
    v = xs[:, 2 * RW_W:3 * RW_W]
    lo = xs[:, LORA_OFF:LORA_OFF + 128]
    g_lo = xs[:, GATE_OFF:GATE_OFF + 128]
    w_raw = -_softplus(-(w0_ref[...] + _mm(jnp.tanh(lo), w2_ref[...]))) - 0.5
    lw_ref[...] = -jnp.exp(w_raw)
    a = _sigmoid(a0_ref[...] + _mm(lo, a2_ref[...]))
    g_ref[...] = _mm(_sigmoid(g_lo), g2_ref[...])
    kk = k * kkw_ref[...]
    norm = jnp.sqrt(_mm_exact_rhs(kk * kk, seg_ref[...]))
    kap = kk / jnp.maximum(norm, 1e-12)
    k2 = k * (1.0 + (a - 1.0) * ka_ref[...])
    r_ref[...] = r
    k_ref[...] = k2
    v_ref[...] = v
    kap_ref[...] = kap
    b_ref[...] = kap * a
    bonus_ref[...] = _mm_exact_rhs(r * k2 * rk_ref[...], seg_ref[...]) * v


def _rwkv_pre(rw, first, period, p, tm):
    m = rw.shape[0]
    row = lambda w: pl.BlockSpec((tm, w), lambda i: (i, 0))
    vec = _full((1, RW_W))
    out = jax.ShapeDtypeStruct((m, RW_W), F32)
    first_spec = pl.BlockSpec((1,) + first.shape[1:], lambda i: (i, 0, 0))
    return pl.pallas_call(
        functools.partial(_rwkv_pre_kernel, period=period),
        grid=(m // tm,),
        in_specs=[row(RW_PROJ_W), first_spec, _full((1, RW_PROJ_W)), vec, vec, vec, vec, vec,
                  _full((128, RW_W)), _full((128, RW_W)), _full((128, RW_W)), _full((RW_W, RW_W))],
        out_specs=[row(RW_W)] * 8,
        out_shape=[out] * 8,
        compiler_params=_cparams(("parallel",)),
        name="rwkv_pre",
    )(rw, first, p['mu'], p['w0'], p['a0'], p['kk'], p['ka'], p['rk'],
      p['w2pad'], p['a2pad'], p['g2'], p['seg'])


def _stack(x):
    lane = lax.broadcasted_iota(jnp.int32, x.shape, 1)
    return jnp.concatenate([jnp.where(lane < RW_HEAD, x, 0.0),
                            jnp.where(lane >= RW_HEAD, x, 0.0)], axis=0)


def _each(fn, *lists):
    return [fn(*xs) for xs in zip(*lists)]


def _unit_lower_inverses(lmats, c):
    n = 2 * c
    ri = lax.broadcasted_iota(jnp.int32, (n, n), 0)
    ci = lax.broadcasted_iota(jnp.int32, (n, n), 1)
    eye = jnp.where(ri == ci, 1.0, 0.0)
    same = lambda s: (ri // s) == (ci // s)
    m1 = _each(lambda l: jnp.where(same(INV_BASE), -l, 0.0), lmats)
    m2 = _each(_mm, m1, m1)
    m4 = _each(_mm, m2, m2)
    t = _each(lambda a, b: eye + a + b + _mm(a, b), m1, m2)
    t = _each(lambda a, b: a + _mm(a, b), t, m4)
    s = INV_BASE
    while s < c:
        keep = same(2 * s) & jnp.logical_not(same(s))
        x = _each(lambda l, a: _mm(jnp.where(keep, l, 0.0), a), lmats, t)
        t = _each(lambda a, b: a - _mm(a, b), t, x)
        s *= 2
    return t


def _wkv_scan_kernel(r_ref, lw_ref, k_ref, v_ref, kap_ref, b_ref, s0_ref, tri_ref,
                     y_ref, sout_ref, st_sc, *, c):
    ci = pl.program_id(1)
    n = 2 * c

    @pl.when(ci == 0)
    def _():
        zero = jnp.zeros((RW_HEAD, RW_HEAD), F32)
        for p in range(RW_PAIRS):
            top = jnp.concatenate([s0_ref[0, 2 * p], zero], axis=1)
            bot = jnp.concatenate([zero, s0_ref[0, 2 * p + 1]], axis=1)
            st_sc[p] = jnp.concatenate([top, bot], axis=0)

    ri = lax.broadcasted_iota(jnp.int32, (n, n), 0)
    cj = lax.broadcasted_iota(jnp.int32, (n, n), 1)
    same_head = (ri // c) == (cj // c)
    strict = same_head & (ri > cj)
    incl = same_head & (ri >= cj)
    tri = tri_ref[...]

    sls = [slice(p * LANES, (p + 1) * LANES) for p in range(RW_PAIRS)]
    load = lambda ref: [ref[:, sl] for sl in sls]
    lw, kap, r, k, b, v = (load(ref) for ref in (lw_ref, kap_ref, r_ref, k_ref, b_ref, v_ref))
    dup = lambda x: jnp.concatenate([x, x], axis=0)
    cs = _each(lambda x: _mm_exact_lhs(tri, x), lw)
    cs_end = [x[c - 1:c, :] for x in cs]
    kh = _each(lambda a, x, w: a * jnp.exp(x - w), kap, cs, lw)
    rt = _each(lambda a, x: a * jnp.exp(x), r, cs)
    e_out = [jnp.exp(-x) for x in cs]
    kt = _each(jnp.multiply, k, e_out)
    bt = _each(jnp.multiply, b, e_out)
    e_end = _each(lambda xe, x: jnp.exp(xe - x), cs_end, cs)
    k_end = _each(jnp.multiply, k, e_end)
    b_end = _each(jnp.multiply, b, e_end)

    khs, rts = _each(_stack, kh), _each(_stack, rt)
    bb, kk = _each(dup, bt), _each(dup, kt)
    lmat = _each(lambda x, y: jnp.where(strict, _mm_nt(x, y), 0.0), khs, bb)
    a_kk = _each(lambda x, y: jnp.where(strict, _mm_nt(x, y), 0.0), khs, kk)
    a_rb = _each(lambda x, y: jnp.where(incl, _mm_nt(x, y), 0.0), rts, bb)
    a_rk = _each(lambda x, y: jnp.where(incl, _mm_nt(x, y), 0.0), rts, kk)
    st = [st_sc[p] for p in range(RW_PAIRS)]
    vs = _each(_stack, v)
    rhs = _each(lambda x, s, a, y: _stack(_mm_nt(x, s)) + _mm(a, y), kh, st, a_kk, vs)
    y0 = _each(lambda x, s, a, y: _stack(_mm_nt(x, s)) + _mm(a, y), rt, st, a_rk, vs)
    tinv = _unit_lower_inverses(lmat, c)
    u = _each(_mm, tinv, rhs)
    ys = _each(lambda y, a, x: y - _mm(a, x), y0, a_rb, u)
    vu = _each(lambda x, y: jnp.concatenate([x, y], axis=0), vs, u)
    kb = _each(lambda x, y: jnp.concatenate([_stack(x), -_stack(y)], axis=0), k_end, b_end)
    st_new = _each(lambda s, xe, x, y: s * jnp.exp(xe) + _mm(x.T, y), st, cs_end, vu, kb)
    for p, sl in enumerate(sls):
        y_ref[:, sl] = ys[p][:c] + ys[p][c:]
        st_sc[p] = st_new[p]

    @pl.when(ci == pl.num_programs(1) - 1)
    def _():
        for p in range(RW_PAIRS):
            st = st_sc[p]
            sout_ref[0, 2 * p] = st[:RW_HEAD, :RW_HEAD]
            sout_ref[0, 2 * p + 1] = st[RW_HEAD:, RW_HEAD:]


def _wkv_scan(r, lw, k, v, kap, b, s0, batch, seq, c):
    nc = seq // c
    tile = pl.BlockSpec((c, RW_W), lambda bi, ci: (bi * nc + ci, 0))
    sspec = pl.BlockSpec((1, RW_HEADS, RW_HEAD, RW_HEAD), lambda bi, ci: (bi, 0, 0, 0))
    tri = jnp.tril(jnp.ones((c, c), F32)).astype(BF16)
    return pl.pallas_call(
        functools.partial(_wkv_scan_kernel, c=c),
        grid=(batch, nc),
        in_specs=[tile] * 6 + [sspec, _full((c, c))],
        out_specs=[tile, sspec],
        out_shape=[jax.ShapeDtypeStruct((batch * seq, RW_W), F32),
                   jax.ShapeDtypeStruct((batch, RW_HEADS, RW_HEAD, RW_HEAD), F32)],
        scratch_shapes=[pltpu.VMEM((RW_PAIRS, LANES, LANES), F32)],
        compiler_params=_cparams(("parallel", "arbitrary")),
        name="wkv_scan",
    )(r, lw, k, v, kap, b, s0, tri)


def _mix_out_kernel(x_ref, att_ref, y_ref, bonus_ref, g_ref, lnw_ref, lnb_ref, seg_ref,
                    woa_ref, wor_ref, gpost_ref, o_ref):
    y = y_ref[...]
    seg = seg_ref[...]
    mean = _mm_exact_rhs(y, seg) * (1.0 / RW_HEAD)
    d = y - mean
    var = _mm_exact_rhs(d * d, seg) * (1.0 / RW_HEAD)
    yn = d * lax.rsqrt(var + GN_EPS) * lnw_ref[...] + lnb_ref[...]
    rw_out = (yn + bonus_ref[...]) * g_ref[...]
    mix = _mm(att_ref[...], woa_ref[...]) + _mm(rw_out, wor_ref[...])
    o_ref[...] = x_ref[...] + _rms(mix, gpost_ref[...], RMS_EPS)


def _mix_out(x, att, y, bonus, g, p, tm):
    m = x.shape[0]
    row = lambda w: pl.BlockSpec((tm, w), lambda i: (i, 0))
    vec = _full((1, RW_W))
    return pl.pallas_call(
        _mix_out_kernel,
        grid=(m // tm,),
        in_specs=[row(D_MODEL), row(ATT_W), row(RW_W), row(RW_W), row(RW_W), vec, vec,
                  _full((RW_W, RW_W)), _full((ATT_W, D_MODEL)), _full((RW_W, D_MODEL)),
                  _full((1, D_MODEL))],
        out_specs=row(D_MODEL),
        out_shape=jax.ShapeDtypeStruct((m, D_MODEL), F32),
        compiler_params=_cparams(("parallel",)),
        name="mix_out",
    )(x, att, y, bonus, g, p['lnx_w'], p['lnx_b'], p['seg'], p['w_out_att'], p['w_out_rw'],
      p['norm_mix_post'])


def _ffn_kernel(x_ref, gpre_ref, gpost_ref, wg_ref, wu_ref, wd_ref, o_ref, f_sc, acc_sc):
    j = pl.program_id(1)

    @pl.when(j == 0)
    def _():
        f_sc[...] = _rms(x_ref[...], gpre_ref[...], RMS_EPS).astype(BF16)
        acc_sc[...] = jnp.zeros(acc_sc.shape, F32)

    f = f_sc[...]
    gate = jnp.dot(f, wg_ref[...], preferred_element_type=F32)
    up = jnp.dot(f, wu_ref[...], preferred_element_type=F32)
    h = gate * _sigmoid(gate) * up
    acc_sc[...] += jnp.dot(h.astype(BF16), wd_ref[...], preferred_element_type=F32)

    @pl.when(j == pl.num_programs(1) - 1)
    def _():
        o_ref[...] = x_ref[...] + _rms(acc_sc[...], gpost_ref[...], RMS_EPS)


def _ffn(x, p, tm, tff):
    m = x.shape[0]
    row = pl.BlockSpec((tm, D_MODEL), lambda i, j: (i, 0))
    vec = pl.BlockSpec((1, D_MODEL), lambda i, j: (0, 0))
    return pl.pallas_call(
        _ffn_kernel,
        grid=(m // tm, D_FF // tff),
        in_specs=[row, vec, vec,
                  pl.BlockSpec((D_MODEL, tff), lambda i, j: (0, j)),
                  pl.BlockSpec((D_MODEL, tff), lambda i, j: (0, j)),
                  pl.BlockSpec((tff, D_MODEL), lambda i, j: (j, 0))],
        out_specs=row,
        out_shape=jax.ShapeDtypeStruct((m, D_MODEL), F32),
        scratch_shapes=[pltpu.VMEM((tm, D_MODEL), BF16), pltpu.VMEM((tm, D_MODEL), F32)],
        compiler_params=_cparams(("parallel", "arbitrary")),
        name="ffn",
    )(x, p['norm_ffn_pre'], p['norm_ffn_post'], p['ffn_gate'], p['ffn_up'], p['ffn_down'])


def _layer_params(l, w):
    row = lambda a: a[l].reshape(1, -1).astype(F32)
    zeros64 = jnp.zeros((64, RW_W), F32)
    head = jnp.arange(RW_W) // RW_HEAD
    return {
        'norm_mix_pre': row(w['norm_mix_pre']), 'norm_mix_post': row(w['norm_mix_post']),
        'norm_ffn_pre': row(w['norm_ffn_pre']), 'norm_ffn_post': row(w['norm_ffn_post']),
        'w_in': w['w_in'][l].astype(BF16),
        'w_out_att': w['w_out'][l][:ATT_W].astype(BF16),
        'w_out_rw': w['w_out'][l][ATT_W:].astype(BF16),
        'lam': [row(w[n]) for n in ('lambda_q1', 'lambda_k1', 'lambda_q2', 'lambda_k2')],
        'subln_w': row(w['subln_w']),
        'mu': row(w['rw_mu']), 'w0': row(w['rw_w0']), 'a0': row(w['rw_a0']),
        'kk': row(w['rw_kk']), 'ka': row(w['rw_ka']), 'rk': row(w['rw_rk']),
        'lnx_w': row(w['rw_lnx_w']), 'lnx_b': row(w['rw_lnx_b']),
        'w2pad': jnp.concatenate([w['rw_w2'][l], zeros64], axis=0).astype(BF16),
        'a2pad': jnp.concatenate([zeros64, w['rw_a2'][l]], axis=0).astype(BF16),
        'g2': w['rw_g2'][l].astype(BF16),
        'seg': (head[:, None] == head[None, :]).astype(BF16),
        'ffn_gate': w['ffn_gate'][l].astype(BF16), 'ffn_up': w['ffn_up'][l].astype(BF16),
        'ffn_down': w['ffn_down'][l].astype(BF16),
    }


def _rwkv_group(rw, shift0, wkv0, p, batch, seq):
    m = batch * seq
    if seq >= WKV_CHUNK:
        tm = min(512, seq)
        tiles = seq // tm
        last_rows = rw.reshape(batch, tiles, tm, RW_PROJ_W)[:, :, -1, :]
        lead = jnp.concatenate([shift0[:, None, :], last_rows[:, :-1, :]], axis=1)
        first = lead.reshape(m // tm, 1, RW_PROJ_W)
        period, c, pad = tm, WKV_CHUNK, 0
    else:
        tm = m
        first = jnp.zeros((batch, seq, RW_PROJ_W), F32).at[:, 0, :].set(shift0).reshape(1, m, RW_PROJ_W)
        period, c = seq, 8
        pad = c - seq
    r, lw, k, v, kap, b, g, bonus = _rwkv_pre(rw, first, period, p, tm)
    scan_in = (r, lw, k, v, kap, b)
    if pad:
        scan_in = [jnp.pad(t.reshape(batch, seq, RW_W), ((0, 0), (0, pad), (0, 0))).reshape(batch * c, RW_W)
                   for t in scan_in]
        y, s_new = _wkv_scan(*scan_in, wkv0, batch, c, c)
        y = y.reshape(batch, c, RW_W)[:, :seq].reshape(m, RW_W)
    else:
        y, s_new = _wkv_scan(*scan_in, wkv0, batch, seq, c)
    return y, bonus, g, s_new


def _layer(x, shift0, wkv0, attend, p, batch, seq):
    m = batch * seq
    tm = min(512, m)
    q, k, v, rw = _inproj(x, p['norm_mix_pre'], p['w_in'], tm)
    att = attend(q, k, v)
    y, bonus, g, wkv_new = _rwkv_group(rw, shift0, wkv0, p, batch, seq)
    x1 = _mix_out(x, att, y, bonus, g, p, tm)
    out = _ffn(x1, p, tm, D_FF // 2)
    shift_new = rw.reshape(batch, seq, RW_PROJ_W)[:, -1]
    return out, k, v, wkv_new, shift_new


def kernel(x_prompt, x_sample, cache_k, cache_v, state_wkv, state_shift, page_table, norm_mix_pre, norm_mix_post, norm_ffn_pre, norm_ffn_post, w_in, w_out, lambda_q1, lambda_k1, lambda_q2, lambda_k2, subln_w, rw_mu, rw_w0, rw_w2, rw_a0, rw_a2, rw_g2, rw_kk, rw_ka, rw_rk, rw_lnx_w, rw_lnx_b, ffn_gate, ffn_up, ffn_down):
    weights = dict(
        norm_mix_pre=norm_mix_pre, norm_mix_post=norm_mix_post, norm_ffn_pre=norm_ffn_pre,
        norm_ffn_post=norm_ffn_post, w_in=w_in, w_out=w_out, lambda_q1=lambda_q1,
        lambda_k1=lambda_k1, lambda_q2=lambda_q2, lambda_k2=lambda_k2, subln_w=subln_w,
        rw_mu=rw_mu, rw_w0=rw_w0, rw_w2=rw_w2, rw_a0=rw_a0, rw_a2=rw_a2, rw_g2=rw_g2,
        rw_kk=rw_kk, rw_ka=rw_ka, rw_rk=rw_rk, rw_lnx_w=rw_lnx_w, rw_lnx_b=rw_lnx_b,
        ffn_gate=ffn_gate, ffn_up=ffn_up, ffn_down=ffn_down)
    depth = w_in.shape[0]
    batch, seq, _ = x_prompt.shape
    db, dseq, _ = x_sample.shape
    n_pool, page = cache_k.shape[1], cache_k.shape[2]
    yp = x_prompt.reshape(batch * seq, D_MODEL)
    ys = x_sample.reshape(db * dseq, D_MODEL)
    outs = [[] for _ in range(8)]
    for l in range(depth):
        lam_init = 0.8 - 0.6 * math.exp(-0.3 * l)
        p = _layer_params(l, weights)
        ck = cache_k[l].reshape(n_pool, page * ATT_HEADS, LANES)
        cv = cache_v[l].reshape(n_pool, page * ATT_HEADS, LANES)

        def attend_p(q, k, v):
            return _attn_prompt(q, k, v, p['lam'], p['subln_w'], batch, seq, lam_init)

        def attend_s(q, k, v):
            t3 = lambda a: a.reshape(db, dseq, ATT_W)
            o = _attn_paged(t3(q), t3(k), t3(v), ck, cv, page_table, p['lam'], p['subln_w'], lam_init)
            return o.reshape(db * dseq, ATT_W)

        zshift = jnp.zeros((batch, RW_PROJ_W), F32)
        zwkv = jnp.zeros((batch, RW_HEADS, RW_HEAD, RW_HEAD), F32)
        yp, kp, vp, wp, sp = _layer(yp, zshift, zwkv, attend_p, p, batch, seq)
        ys, ks, vs, ws, ss = _layer(ys, state_shift[l], state_wkv[l], attend_s, p, db, dseq)
        kv_p = lambda a: a.reshape(batch, seq, ATT_HEADS, ATT_V_DIM)
        kv_s = lambda a: a.reshape(db, dseq, ATT_HEADS, ATT_V_DIM)
        for lst, val in zip(outs, (kv_p(kp), kv_p(vp), wp, sp, kv_s(ks), kv_s(vs), ws, ss)):
            lst.append(val)
    stacked = [jnp.stack(o) for o in outs]
    return (yp.reshape(batch, seq, D_MODEL), ys.reshape(db, dseq, D_MODEL), *stacked)
```

```python
import functools
import math

import jax
import jax.numpy as jnp
from jax import lax
from jax.experimental import pallas as pl
from jax.experimental.pallas import tpu as pltpu

F32 = jnp.float32
BF16 = jnp.bfloat16

D_MODEL = 1024
ATT_W = 512
RW_W = 512
ATT_HEADS = 4
ATT_QK_DIM = 64
ATT_V_DIM = 128
RW_HEAD = 64
RW_HEADS = 8
RW_PAIRS = RW_HEADS // 2
LORA_OFF = 3 * RW_W
GATE_OFF = LORA_OFF + 128
RW_PROJ_W = 1792
IN_W = 3 * ATT_W + RW_PROJ_W
D_FF = 2816
RMS_EPS = 1e-6
SUBLN_EPS = 1e-5
GN_EPS = 64e-5
NEG_BIG = -1e30

LANES = 128
VMEM_LIMIT = 56 * 1024 * 1024

WKV_CHUNK = 64
WKV_SEQS = 4
INV_BASE = 8


def _cparams(sem):
    return pltpu.CompilerParams(dimension_semantics=sem, vmem_limit_bytes=VMEM_LIMIT)


def _full(shape):
    return pl.BlockSpec(shape, lambda *_: (0,) * len(shape))


def _mm(a, b):
    return jnp.dot(a.astype(BF16), b.astype(BF16), preferred_element_type=F32)


def _mm_nt(a, b):
    return lax.dot_general(a.astype(BF16), b.astype(BF16), (((1,), (1,)), ((), ())),
                           preferred_element_type=F32)


def _split2(x):
    hi = x.astype(BF16)
    lo = (x - hi.astype(F32)).astype(BF16)
    return hi, lo


def _split3(x):
    hi = x.astype(BF16)
    r1 = x - hi.astype(F32)
    mid = r1.astype(BF16)
    lo = (r1 - mid.astype(F32)).astype(BF16)
    return hi, mid, lo


def _mm_exact_rhs(a, b_bf16):
    h1, h2, h3 = _split3(a)
    d = functools.partial(jnp.dot, preferred_element_type=F32)
    return d(h1, b_bf16) + d(h2, b_bf16) + d(h3, b_bf16)


def _mm_exact_lhs(a_bf16, b):
    h1, h2, h3 = _split3(b)
    d = functools.partial(jnp.dot, preferred_element_type=F32)
    return d(a_bf16, h1) + d(a_bf16, h2) + d(a_bf16, h3)


def _rms(x, g, eps):
    return x * lax.rsqrt(jnp.mean(x * x, axis=-1, keepdims=True) + eps) * g


def _diff_lambda(lq1, lk1, lq2, lk2, lam_init):
    s1 = jnp.sum(lq1[...] * lk1[...], axis=-1, keepdims=True)
    s2 = jnp.sum(lq2[...] * lk2[...], axis=-1, keepdims=True)
    return jnp.exp(s1) - jnp.exp(s2) + lam_init


def _inproj_kernel(x_ref, g_ref, w_ref, q_ref, k_ref, v_ref, rw_ref, k4_ref, v4_ref):
    h = _rms(x_ref[...], g_ref[...], RMS_EPS).astype(BF16)
    tm = h.shape[0]
    dot = functools.partial(jnp.dot, preferred_element_type=F32)
    q_ref[...] = dot(h, w_ref[:, 0:ATT_W])
    k = dot(h, w_ref[:, ATT_W:2 * ATT_W])
    v = dot(h, w_ref[:, 2 * ATT_W:3 * ATT_W])
    k_ref[...] = k
    v_ref[...] = v
    for hd in range(ATT_HEADS):
        sl = slice(hd * LANES, (hd + 1) * LANES)
        k4_ref[pl.ds(hd, tm, stride=ATT_HEADS), :] = k[:, sl]
        v4_ref[pl.ds(hd, tm, stride=ATT_HEADS), :] = v[:, sl]
    rw_ref[...] = dot(h, w_ref[:, 3 * ATT_W:])


def _inproj(x, g, w_in_bf16, tm):
    m = x.shape[0]
    row = lambda w: pl.BlockSpec((tm, w), lambda i: (i, 0))
    heads_row = pl.BlockSpec((tm * ATT_HEADS, LANES), lambda i: (i, 0))
    return pl.pallas_call(
        _inproj_kernel,
        grid=(m // tm,),
        in_specs=[row(D_MODEL), _full((1, D_MODEL)), _full((D_MODEL, IN_W))],
        out_specs=[row(ATT_W), row(ATT_W), row(ATT_W), row(RW_PROJ_W), heads_row, heads_row],
        out_shape=[jax.ShapeDtypeStruct((m, ATT_W), F32)] * 3
        + [jax.ShapeDtypeStruct((m, RW_PROJ_W), F32)]
        + [jax.ShapeDtypeStruct((m * ATT_HEADS, LANES), F32)] * 2,
        compiler_params=_cparams(("parallel",)),
        name="inproj",
    )(x, g, w_in_bf16)


ATT_GRP = 128
ATT_ONES = 16
LOG2E = 1.4426950408889634


def _attn_prompt_kernel(lq1, lk1, lq2, lk2, sw_ref, q_ref, k_ref, v_ref, o_ref, kb_sc, vt_sc,
                        *, tq, lam_init):
    i = pl.program_id(2)
    groups = tq // ATT_GRP
    n_blk = k_ref.shape[0] // tq

    @pl.when(i == 0)
    def _():
        kb_sc[...] = k_ref[...].astype(BF16)
        ones = jnp.ones((ATT_ONES, tq), BF16)
        for jb in range(n_blk):
            vt_sc[jb] = jnp.concatenate([v_ref[jb * tq:(jb + 1) * tq, :].T.astype(BF16), ones], axis=0)

    lam = _diff_lambda(lq1, lk1, lq2, lk2, lam_init)
    lane = lax.broadcasted_iota(jnp.int32, (ATT_GRP, LANES), 1)
    q_maps = []
    for g in range(groups):
        q = q_ref[g * ATT_GRP:(g + 1) * ATT_GRP, :] * (ATT_QK_DIM ** -0.5 * LOG2E)
        q_maps.append(jnp.concatenate([jnp.where(lane < ATT_QK_DIM, q, 0.0),
                                       jnp.where(lane >= ATT_QK_DIM, q, 0.0)], axis=0).astype(BF16))

    def update(ss, vts, carry):
        ms, accs = carry
        m_new = _each(lambda m, s: jnp.maximum(m, jnp.max(s, axis=0, keepdims=True)), ms, ss)
        alpha = _each(lambda m, mn: jnp.exp2(m - mn), ms, m_new)
        ps = _each(lambda s, mn: jnp.exp2(s - mn).astype(BF16), ss, m_new)
        pv = _each(lambda vt, p: jnp.dot(vt, p, preferred_element_type=F32), vts, ps)
        accs = _each(lambda a, acc, x: a * acc + x, alpha, accs, pv)
        return tuple(m_new), tuple(accs)

    def full_block(j, carry):
        kb = kb_sc[pl.ds(pl.multiple_of(j * tq, tq), tq), :]
        vt = vt_sc[j]
        ss = [lax.dot_general(kb, qm, (((1,), (1,)), ((), ())), preferred_element_type=F32)
              for qm in q_maps]
        return update(ss, [vt] * len(q_maps), carry)

    w2 = 2 * ATT_GRP
    init = ((jnp.full((1, w2), NEG_BIG, F32),) * groups,
            (jnp.zeros((ATT_V_DIM + ATT_ONES, w2), F32),) * groups)
    carry = lax.fori_loop(0, i, full_block, init)

    start = pl.multiple_of(i * tq, tq)
    vt = vt_sc[i]
    ss, vts = [], []
    for g, qm in enumerate(q_maps):
        keys = (g + 1) * ATT_GRP
        kb = kb_sc[pl.ds(start, keys), :]
        s = lax.dot_general(kb, qm, (((1,), (1,)), ((), ())), preferred_element_type=F32)
        key = lax.broadcasted_iota(jnp.int32, (keys, w2), 0)
        qry = lax.broadcasted_iota(jnp.int32, (keys, w2), 1) % ATT_GRP + (keys - ATT_GRP)
        ss.append(jnp.where(key <= qry, s, -jnp.inf))
        vts.append(vt[:, :keys])
    _, accs = update(ss, vts, carry)
    for g in range(groups):
        z = accs[g][:ATT_V_DIM] / accs[g][ATT_V_DIM:ATT_V_DIM + 1]
        o = (z[:, :ATT_GRP] - lam * z[:, ATT_GRP:]).T
        o_ref[g * ATT_GRP:(g + 1) * ATT_GRP, :] = _rms(o, sw_ref[...], SUBLN_EPS) * (1.0 - lam_init)


def _attn_prompt(q, k, v, lam_params, subln_w, batch, seq, lam_init, tq=512):
    tq = min(tq, seq)
    nq = seq // tq
    lam_specs = [_full((1, ATT_QK_DIM))] * 4
    qspec = pl.BlockSpec((tq, LANES), lambda b, h, i: (b * nq + i, h))
    kvspec = pl.BlockSpec((seq, LANES), lambda b, h, i: (b, h))
    return pl.pallas_call(
        functools.partial(_attn_prompt_kernel, tq=tq, lam_init=lam_init),
        grid=(batch, ATT_HEADS, nq),
        in_specs=lam_specs + [_full((1, ATT_V_DIM)), qspec, kvspec, kvspec],
        out_specs=qspec,
        out_shape=jax.ShapeDtypeStruct((batch * seq, ATT_W), F32),
        scratch_shapes=[pltpu.VMEM((seq, LANES), BF16),
                        pltpu.VMEM((nq, ATT_V_DIM + ATT_ONES, tq), BF16)],
        compiler_params=_cparams(("parallel", "parallel", "arbitrary")),
        name="attn_prompt",
    )(*lam_params, subln_w, q, k, v)


def _attn_paged_kernel(pt_ref, lq1, lk1, lq2, lk2, sw_ref, q_ref, kn_ref, vn_ref, *rest,
                       pages_per_step, n_tok, lam_init):
    del pt_ref
    k_refs = rest[:pages_per_step]
    v_refs = rest[pages_per_step:2 * pages_per_step]
    o_ref, m_sc, l_sc, acc_sc = rest[2 * pages_per_step:]
    c = pl.program_id(1)
    half = n_tok * ATT_HEADS
    rows = 2 * half
    page_rows = k_refs[0].shape[1]

    @pl.when(c == 0)
    def _():
        m_sc[...] = jnp.full(m_sc.shape, NEG_BIG, F32)
        l_sc[...] = jnp.zeros(l_sc.shape, F32)
        acc_sc[...] = jnp.zeros(acc_sc.shape, F32)

    row = lax.broadcasted_iota(jnp.int32, (rows, LANES), 0)
    lane = lax.broadcasted_iota(jnp.int32, (rows, LANES), 1)
    qs = jnp.where((lane // ATT_QK_DIM) == (row // half), q_ref[0], 0.0) * (ATT_QK_DIM ** -0.5)
    qs_b = qs.astype(BF16)
    srow = lax.broadcasted_iota(jnp.int32, (rows, page_rows), 0)
    scol = lax.broadcasted_iota(jnp.int32, (rows, page_rows), 1)
    same_head = (scol % ATT_HEADS) == (srow % ATT_HEADS)

    ss = [jnp.where(same_head, _mm_nt(qs_b, kr[0]), -jnp.inf) for kr in k_refs]
    m_old, l_old, acc_old = m_sc[...], l_sc[...], acc_sc[...]
    m = jnp.maximum(m_old, jnp.max(functools.reduce(jnp.maximum, ss), axis=-1, keepdims=True))
    alpha = jnp.exp(m_old - m)
    ps = [jnp.exp(s - m) for s in ss]
    l = alpha * l_old + jnp.sum(functools.reduce(jnp.add, ps), axis=-1, keepdims=True)
    pv = functools.reduce(jnp.add, [_mm(p, vr[0]) for p, vr in zip(ps, v_refs)])
    acc = alpha * acc_old + pv
    m_sc[...], l_sc[...], acc_sc[...] = m, l, acc

    @pl.when(c == pl.num_programs(1) - 1)
    def _():
        lam = _diff_lambda(lq1, lk1, lq2, lk2, lam_init)
        mm, ll, aa = m, l, acc
        r1 = lax.broadcasted_iota(jnp.int32, (rows, 1), 0)
        tok = (r1 // ATT_HEADS) % n_tok
        for j in range(n_tok):
            s = jnp.sum(qs * kn_ref[0, j], axis=-1, keepdims=True)
            s = jnp.where(tok >= j, s, -jnp.inf)
            m_new = jnp.maximum(mm, s)
            alpha = jnp.exp(mm - m_new)
            p = jnp.exp(s - m_new)
            ll = alpha * ll + p
            aa = alpha * aa + p * vn_ref[0, j]
            mm = m_new
        z = aa / ll
        o = z[:half] - lam * z[half:]
        o_ref[0] = _rms(o, sw_ref[...], SUBLN_EPS) * (1.0 - lam_init)


def _attn_paged(q, kn, vn, cache_k, cache_v, page_table, lam_params, subln_w, lam_init,
                pages_per_step=32):
    db, n_tok, _ = q.shape
    n_pages = page_table.shape[1]
    page_rows = cache_k.shape[1]
    pages_per_step = math.gcd(pages_per_step, n_pages)
    steps = n_pages // pages_per_step
    half = n_tok * ATT_HEADS
    rows = 2 * half
    pt_flat = page_table.reshape(-1)
    heads = lambda a: a.reshape(db, n_tok, ATT_HEADS, LANES)
    q_rows = jnp.broadcast_to(heads(q)[:, None], (db, 2, n_tok, ATT_HEADS, LANES)).reshape(db, rows, LANES)
    rep = lambda a: jnp.broadcast_to(
        heads(a)[:, :, None, None], (db, n_tok, 2, n_tok, ATT_HEADS, LANES)).reshape(db, n_tok, rows, LANES)

    def page_spec(i):
        return pl.BlockSpec(
            (1, page_rows, LANES),
            lambda b, c, pt: (pt[b * n_pages + c * pages_per_step + i], 0, 0))

    const = lambda shape: pl.BlockSpec(shape, lambda b, c, pt: (0,) * len(shape))
    new_spec = pl.BlockSpec((1, n_tok, rows, LANES), lambda b, c, pt: (b, 0, 0, 0))
    grid_spec = pltpu.PrefetchScalarGridSpec(
        num_scalar_prefetch=1,
        grid=(db, steps),
        in_specs=[const((1, ATT_QK_DIM))] * 4 + [const((1, ATT_V_DIM))]
        + [pl.BlockSpec((1, rows, LANES), lambda b, c, pt: (b, 0, 0)), new_spec, new_spec]
        + [page_spec(i) for i in range(pages_per_step)] * 2,
        out_specs=pl.BlockSpec((1, half, LANES), lambda b, c, pt: (b, 0, 0)),
        scratch_shapes=[pltpu.VMEM((rows, 1), F32), pltpu.VMEM((rows, 1), F32),
                        pltpu.VMEM((rows, LANES), F32)],
    )
    out = pl.pallas_call(
        functools.partial(_attn_paged_kernel, pages_per_step=pages_per_step, n_tok=n_tok,
                          lam_init=lam_init),
        grid_spec=grid_spec,
        out_shape=jax.ShapeDtypeStruct((db, half, LANES), F32),
        compiler_params=_cparams(("parallel", "arbitrary")),
        name="attn_paged",
    )(pt_flat, *lam_params, subln_w, q_rows, rep(kn), rep(vn),
      *([cache_k] * pages_per_step), *([cache_v] * pages_per_step))
    return out.reshape(db, n_tok, ATT_W)


def _softplus(x):
    return jnp.maximum(x, 0.0) + jnp.log1p(jnp.exp(-jnp.abs(x)))


def _sigmoid(x):
    return 1.0 / (1.0 + jnp.exp(-x))


def _rwkv_pre_kernel(rw_ref, first_ref, mu_ref, w0_ref, a0_ref, kkw_ref, ka_ref, rk_ref,
                     w2_ref, a2_ref, g2_ref, seg_ref,
                     r_ref, lw_ref, k_ref, v_ref, kap_ref, b_ref, g_ref, bonus_ref,
                     *, period):
    x = rw_ref[...]
    tm = x.shape[0]
    row = lax.broadcasted_iota(jnp.int32, x.shape, 0)
    prev = pltpu.roll(x, 1, axis=0)
    prev = jnp.where(row % period == 0, first_ref[0], prev)
    xs = x + (prev - x) * mu_ref[...]
    r = xs[:, 0:RW_W]
    k = xs[:, RW_W:2 * RW_W]
    v = xs[:, 2 * RW_W:3 * RW_W]
    lo = xs[:, LORA_OFF:LORA_OFF + 128]
    g_lo = xs[:, GATE_OFF:GATE_OFF + 128]
    w_raw = -_softplus(-(w0_ref[...] + _mm(jnp.tanh(lo), w2_ref[...]))) - 0.5
    lw_ref[...] = -jnp.exp(w_raw)
    a = _sigmoid(a0_ref[...] + _mm(lo, a2_ref[...]))
    g_ref[...] = _mm(_sigmoid(g_lo), g2_ref[...])
    kk = k * kkw_ref[...]
    norm = jnp.sqrt(_mm_exact_rhs(kk * kk, seg_ref[...]))
    kap = kk / jnp.maximum(norm, 1e-12)
    k2 = k * (1.0 + (a - 1.0) * ka_ref[...])
    r_ref[...] = r
    k_ref[...] = k2
    v_ref[...] = v
    kap_ref[...] = kap
    b_ref[...] = kap * a
    bonus_ref[...] = _mm_exact_rhs(r * k2 * rk_ref[...], seg_ref[...]) * v


def _rwkv_pre(rw, first, period, p, tm):
    m = rw.shape[0]
    row = lambda w: pl.BlockSpec((tm, w), lambda i: (i, 0))
    vec = _full((1, RW_W))
    out = jax.ShapeDtypeStruct((m, RW_W), F32)
    first_spec = pl.BlockSpec((1,) + first.shape[1:], lambda i: (i, 0, 0))
    return pl.pallas_call(
        functools.partial(_rwkv_pre_kernel, period=period),
        grid=(m // tm,),
        in_specs=[row(RW_PROJ_W), first_spec, _full((1, RW_PROJ_W)), vec, vec, vec, vec, vec,
                  _full((128, RW_W)), _full((128, RW_W)), _full((128, RW_W)), _full((RW_W, RW_W))],
        out_specs=[row(RW_W)] * 8,
        out_shape=[out] * 8,
        compiler_params=_cparams(("parallel",)),
        name="rwkv_pre",
    )(rw, first, p['mu'], p['w0'], p['a0'], p['kk'], p['ka'], p['rk'],
      p['w2pad'], p['a2pad'], p['g2'], p['seg'])


def _stack(x):
    lane = lax.broadcasted_iota(jnp.int32, x.shape, 1)
    return jnp.concatenate([jnp.where(lane < RW_HEAD, x, 0.0),
                            jnp.where(lane >= RW_HEAD, x, 0.0)], axis=0)


def _each(fn, *lists):
    return [fn(*xs) for xs in zip(*lists)]


def _unit_lower_inverses(lmats, c):
    n = 2 * c
    ri = lax.broadcasted_iota(jnp.int32, (n, n), 0)
    ci = lax.broadcasted_iota(jnp.int32, (n, n), 1)
    eye = jnp.where(ri == ci, 1.0, 0.0)
    same = lambda s: (ri // s) == (ci // s)
    m1 = _each(lambda l: jnp.where(same(INV_BASE), -l, 0.0), lmats)
    m2 = _each(_mm, m1, m1)
    m4 = _each(_mm, m2, m2)
    t = _each(lambda a, b: eye + a + b + _mm(a, b), m1, m2)
    t = _each(lambda a, b: a + _mm(a, b), t, m4)
    s = INV_BASE
    while s < c:
        keep = same(2 * s) & jnp.logical_not(same(s))
        x = _each(lambda l, a: _mm(jnp.where(keep, l, 0.0), a), lmats, t)
        t = _each(lambda a, b: a - _mm(a, b), t, x)
        s *= 2
    return t


def _wkv_scan_kernel(r_ref, lw_ref, k_ref, v_ref, kap_ref, b_ref, s0_ref, tri_ref,
                     y_ref, sout_ref, st_sc, *, c, nseq):
    ci = pl.program_id(1)
    n = 2 * c
    problems = [(g, p) for g in range(nseq) for p in range(RW_PAIRS)]

    @pl.when(ci == 0)
    def _():
        zero = jnp.zeros((RW_HEAD, RW_HEAD), F32)
        for i, (g, p) in enumerate(problems):
            top = jnp.concatenate([s0_ref[g, 2 * p], zero], axis=1)
            bot = jnp.concatenate([zero, s0_ref[g, 2 * p + 1]], axis=1)
            st_sc[i] = jnp.concatenate([top, bot], axis=0)

    ri = lax.broadcasted_iota(jnp.int32, (n, n), 0)
    cj = lax.broadcasted_iota(jnp.int32, (n, n), 1)
    same_head = (ri // c) == (cj // c)
    strict = same_head & (ri > cj)
    incl = same_head & (ri >= cj)
    tri = tri_ref[...]

    sls = [slice(p * LANES, (p + 1) * LANES) for p in range(RW_PAIRS)]
    load = lambda ref: [ref[g, 0, :, sls[p]] for g, p in problems]
    lw, kap, r, k, b, v = (load(ref) for ref in (lw_ref, kap_ref, r_ref, k_ref, b_ref, v_ref))
    dup = lambda x: jnp.concatenate([x, x], axis=0)
    cs = _each(lambda x: _mm_exact_lhs(tri, x), lw)
    cs_end = [x[c - 1:c, :] for x in cs]
    kh = _each(lambda a, x, w: a * jnp.exp(x - w), kap, cs, lw)
    rt = _each(lambda a, x: a * jnp.exp(x), r, cs)
    e_out = [jnp.exp(-x) for x in cs]
    kt = _each(jnp.multiply, k, e_out)
    bt = _each(jnp.multiply, b, e_out)
    e_end = _each(lambda xe, x: jnp.exp(xe - x), cs_end, cs)
    k_end = _each(jnp.multiply, k, e_end)
    b_end = _each(jnp.multiply, b, e_end)

    khs, rts = _each(_stack, kh), _each(_stack, rt)
    bb, kk = _each(dup, bt), _each(dup, kt)
    lmat = _each(lambda x, y: jnp.where(strict, _mm_nt(x, y), 0.0), khs, bb)
    a_kk = _each(lambda x, y: jnp.where(strict, _mm_nt(x, y), 0.0), khs, kk)
    a_rb = _each(lambda x, y: jnp.where(incl, _mm_nt(x, y), 0.0), rts, bb)
    a_rk = _each(lambda x, y: jnp.where(incl, _mm_nt(x, y), 0.0), rts, kk)
    st = [st_sc[i] for i in range(len(problems))]
    vs = _each(_stack, v)
    rhs = _each(lambda x, s, a, y: _stack(_mm_nt(x, s)) + _mm(a, y), kh, st, a_kk, vs)
    y0 = _each(lambda x, s, a, y: _stack(_mm_nt(x, s)) + _mm(a, y), rt, st, a_rk, vs)
    tinv = _unit_lower_inverses(lmat, c)
    u = _each(_mm, tinv, rhs)
    ys = _each(lambda y, a, x: y - _mm(a, x), y0, a_rb, u)
    vu = _each(lambda x, y: jnp.concatenate([x, y], axis=0), vs, u)
    kb = _each(lambda x, y: jnp.concatenate([_stack(x), -_stack(y)], axis=0), k_end, b_end)
    st_new = _each(lambda s, xe, x, y: s * jnp.exp(xe) + _mm(x.T, y), st, cs_end, vu, kb)
    for i, (g, p) in enumerate(problems):
        y_ref[g, 0, :, sls[p]] = ys[i][:c] + ys[i][c:]
        st_sc[i] = st_new[i]

    @pl.when(ci == pl.num_programs(1) - 1)
    def _():
        for i, (g, p) in enumerate(problems):
            st = st_sc[i]
            sout_ref[g, 2 * p] = st[:RW_HEAD, :RW_HEAD]
            sout_ref[g, 2 * p + 1] = st[RW_HEAD:, RW_HEAD:]


def _wkv_scan(r, lw, k, v, kap, b, s0, batch, seq, c, nseq=WKV_SEQS):
    nc = seq // c
    nseq = math.gcd(nseq, batch)
    tile = pl.BlockSpec((nseq, 1, c, RW_W), lambda bi, ci: (bi, ci, 0, 0))
    sspec = pl.BlockSpec((nseq, RW_HEADS, RW_HEAD, RW_HEAD), lambda bi, ci: (bi, 0, 0, 0))
    tri = jnp.tril(jnp.ones((c, c), F32)).astype(BF16)
    tiles = [t.reshape(batch, nc, c, RW_W) for t in (r, lw, k, v, kap, b)]
    y, s_new = pl.pallas_call(
        functools.partial(_wkv_scan_kernel, c=c, nseq=nseq),
        grid=(batch // nseq, nc),
        in_specs=[tile] * 6 + [sspec, _full((c, c))],
        out_specs=[tile, sspec],
        out_shape=[jax.ShapeDtypeStruct((batch, nc, c, RW_W), F32),
                   jax.ShapeDtypeStruct((batch, RW_HEADS, RW_HEAD, RW_HEAD), F32)],
        scratch_shapes=[pltpu.VMEM((nseq * RW_PAIRS, LANES, LANES), F32)],
        compiler_params=_cparams(("parallel", "arbitrary")),
        name="wkv_scan",
    )(*tiles, s0, tri)
    return y.reshape(batch * seq, RW_W), s_new


def _mix_out_kernel(x_ref, att_ref, y_ref, bonus_ref, g_ref, lnw_ref, lnb_ref, seg_ref,
                    woa_ref, wor_ref, gpost_ref, o_ref):
    y = y_ref[...]
    seg = seg_ref[...]
    mean = _mm_exact_rhs(y, seg) * (1.0 / RW_HEAD)
    d = y - mean
    var = _mm_exact_rhs(d * d, seg) * (1.0 / RW_HEAD)
    yn = d * lax.rsqrt(var + GN_EPS) * lnw_ref[...] + lnb_ref[...]
    rw_out = (yn + bonus_ref[...]) * g_ref[...]
    mix = _mm(att_ref[...], woa_ref[...]) + _mm(rw_out, wor_ref[...])
    o_ref[...] = x_ref[...] + _rms(mix, gpost_ref[...], RMS_EPS)


def _mix_out(x, att, y, bonus, g, p, tm):
    m = x.shape[0]
    row = lambda w: pl.BlockSpec((tm, w), lambda i: (i, 0))
    vec = _full((1, RW_W))
    return pl.pallas_call(
        _mix_out_kernel,
        grid=(m // tm,),
        in_specs=[row(D_MODEL), row(ATT_W), row(RW_W), row(RW_W), row(RW_W), vec, vec,
                  _full((RW_W, RW_W)), _full((ATT_W, D_MODEL)), _full((RW_W, D_MODEL)),
                  _full((1, D_MODEL))],
        out_specs=row(D_MODEL),
        out_shape=jax.ShapeDtypeStruct((m, D_MODEL), F32),
        compiler_params=_cparams(("parallel",)),
        name="mix_out",
    )(x, att, y, bonus, g, p['lnx_w'], p['lnx_b'], p['seg'], p['w_out_att'], p['w_out_rw'],
      p['norm_mix_post'])


def _ffn_kernel(x_ref, gpre_ref, gpost_ref, wg_ref, wu_ref, wd_ref, o_ref, f_sc, acc_sc):
    j = pl.program_id(1)

    @pl.when(j == 0)
    def _():
        f_sc[...] = _rms(x_ref[...], gpre_ref[...], RMS_EPS).astype(BF16)
        acc_sc[...] = jnp.zeros(acc_sc.shape, F32)

    f = f_sc[...]
    gate = jnp.dot(f, wg_ref[...], preferred_element_type=F32)
    up = jnp.dot(f, wu_ref[...], preferred_element_type=F32)
    h = gate * _sigmoid(gate) * up
    acc_sc[...] += jnp.dot(h.astype(BF16), wd_ref[...], preferred_element_type=F32)

    @pl.when(j == pl.num_programs(1) - 1)
    def _():
        o_ref[...] = x_ref[...] + _rms(acc_sc[...], gpost_ref[...], RMS_EPS)


def _ffn(x, p, tm, tff):
    m = x.shape[0]
    row = pl.BlockSpec((tm, D_MODEL), lambda i, j: (i, 0))
    vec = pl.BlockSpec((1, D_MODEL), lambda i, j: (0, 0))
    return pl.pallas_call(
        _ffn_kernel,
        grid=(m // tm, D_FF // tff),
        in_specs=[row, vec, vec,
                  pl.BlockSpec((D_MODEL, tff), lambda i, j: (0, j)),
                  pl.BlockSpec((D_MODEL, tff), lambda i, j: (0, j)),
                  pl.BlockSpec((tff, D_MODEL), lambda i, j: (j, 0))],
        out_specs=row,
        out_shape=jax.ShapeDtypeStruct((m, D_MODEL), F32),
        scratch_shapes=[pltpu.VMEM((tm, D_MODEL), BF16), pltpu.VMEM((tm, D_MODEL), F32)],
        compiler_params=_cparams(("parallel", "arbitrary")),
        name="ffn",
    )(x, p['norm_ffn_pre'], p['norm_ffn_post'], p['ffn_gate'], p['ffn_up'], p['ffn_down'])


def _layer_params(l, w):
    row = lambda a: a[l].reshape(1, -1).astype(F32)
    zeros64 = jnp.zeros((64, RW_W), F32)
    head = jnp.arange(RW_W) // RW_HEAD
    return {
        'norm_mix_pre': row(w['norm_mix_pre']), 'norm_mix_post': row(w['norm_mix_post']),
        'norm_ffn_pre': row(w['norm_ffn_pre']), 'norm_ffn_post': row(w['norm_ffn_post']),
        'w_in': w['w_in'][l].astype(BF16),
        'w_out_att': w['w_out'][l][:ATT_W].astype(BF16),
        'w_out_rw': w['w_out'][l][ATT_W:].astype(BF16),
        'lam': [row(w[n]) for n in ('lambda_q1', 'lambda_k1', 'lambda_q2', 'lambda_k2')],
        'subln_w': row(w['subln_w']),
        'mu': row(w['rw_mu']), 'w0': row(w['rw_w0']), 'a0': row(w['rw_a0']),
        'kk': row(w['rw_kk']), 'ka': row(w['rw_ka']), 'rk': row(w['rw_rk']),
        'lnx_w': row(w['rw_lnx_w']), 'lnx_b': row(w['rw_lnx_b']),
        'w2pad': jnp.concatenate([w['rw_w2'][l], zeros64], axis=0).astype(BF16),
        'a2pad': jnp.concatenate([zeros64, w['rw_a2'][l]], axis=0).astype(BF16),
        'g2': w['rw_g2'][l].astype(BF16),
        'seg': (head[:, None] == head[None, :]).astype(BF16),
        'ffn_gate': w['ffn_gate'][l].astype(BF16), 'ffn_up': w['ffn_up'][l].astype(BF16),
        'ffn_down': w['ffn_down'][l].astype(BF16),
    }


def _rwkv_group(rw, shift0, wkv0, p, batch, seq):
    m = batch * seq
    if seq >= WKV_CHUNK:
        tm = min(512, seq)
        tiles = seq // tm
        last_rows = rw.reshape(batch, tiles, tm, RW_PROJ_W)[:, :, -1, :]
        lead = jnp.concatenate([shift0[:, None, :], last_rows[:, :-1, :]], axis=1)
        first = lead.reshape(m // tm, 1, RW_PROJ_W)
        period, c, pad = tm, WKV_CHUNK, 0
    else:
        tm = m
        first = jnp.zeros((batch, seq, RW_PROJ_W), F32).at[:, 0, :].set(shift0).reshape(1, m, RW_PROJ_W)
        period, c = seq, 8
        pad = c - seq
    r, lw, k, v, kap, b, g, bonus = _rwkv_pre(rw, first, period, p, tm)
    scan_in = (r, lw, k, v, kap, b)
    if pad:
        scan_in = [jnp.pad(t.reshape(batch, seq, RW_W), ((0, 0), (0, pad), (0, 0))).reshape(batch * c, RW_W)
                   for t in scan_in]
        y, s_new = _wkv_scan(*scan_in, wkv0, batch, c, c)
        y = y.reshape(batch, c, RW_W)[:, :seq].reshape(m, RW_W)
    else:
        y, s_new = _wkv_scan(*scan_in, wkv0, batch, seq, c)
    return y, bonus, g, s_new


def _layer(x, shift0, wkv0, attend, p, batch, seq):
    m = batch * seq
    tm = min(512, m)
    q, k, v, rw, k4, v4 = _inproj(x, p['norm_mix_pre'], p['w_in'], tm)
    att = attend(q, k, v)
    y, bonus, g, wkv_new = _rwkv_group(rw, shift0, wkv0, p, batch, seq)
    x1 = _mix_out(x, att, y, bonus, g, p, tm)
    out = _ffn(x1, p, tm, D_FF // 2)
    shift_new = rw.reshape(batch, seq, RW_PROJ_W)[:, -1]
    return out, k4, v4, wkv_new, shift_new


def kernel(x_prompt, x_sample, cache_k, cache_v, state_wkv, state_shift, page_table, norm_mix_pre, norm_mix_post, norm_ffn_pre, norm_ffn_post, w_in, w_out, lambda_q1, lambda_k1, lambda_q2, lambda_k2, subln_w, rw_mu, rw_w0, rw_w2, rw_a0, rw_a2, rw_g2, rw_kk, rw_ka, rw_rk, rw_lnx_w, rw_lnx_b, ffn_gate, ffn_up, ffn_down):
    weights = dict(
        norm_mix_pre=norm_mix_pre, norm_mix_post=norm_mix_post, norm_ffn_pre=norm_ffn_pre,
        norm_ffn_post=norm_ffn_post, w_in=w_in, w_out=w_out, lambda_q1=lambda_q1,
        lambda_k1=lambda_k1, lambda_q2=lambda_q2, lambda_k2=lambda_k2, subln_w=subln_w,
        rw_mu=rw_mu, rw_w0=rw_w0, rw_w2=rw_w2, rw_a0=rw_a0, rw_a2=rw_a2, rw_g2=rw_g2,
        rw_kk=rw_kk, rw_ka=rw_ka, rw_rk=rw_rk, rw_lnx_w=rw_lnx_w, rw_lnx_b=rw_lnx_b,
        ffn_gate=ffn_gate, ffn_up=ffn_up, ffn_down=ffn_down)
    depth = w_in.shape[0]
    batch, seq, _ = x_prompt.shape
    db, dseq, _ = x_sample.shape
    n_pool, page = cache_k.shape[1], cache_k.shape[2]
    yp = x_prompt.reshape(batch * seq, D_MODEL)
    ys = x_sample.reshape(db * dseq, D_MODEL)
    outs = [[] for _ in range(8)]
    for l in range(depth):
        lam_init = 0.8 - 0.6 * math.exp(-0.3 * l)
        p = _layer_params(l, weights)
        ck = cache_k[l].reshape(n_pool, page * ATT_HEADS, LANES)
        cv = cache_v[l].reshape(n_pool, page * ATT_HEADS, LANES)

        def attend_p(q, k, v):
            return _attn_prompt(q, k, v, p['lam'], p['subln_w'], batch, seq, lam_init)

        def attend_s(q, k, v):
            t3 = lambda a: a.reshape(db, dseq, ATT_W)
            o = _attn_paged(t3(q), t3(k), t3(v), ck, cv, page_table, p['lam'], p['subln_w'], lam_init)
            return o.reshape(db * dseq, ATT_W)

        zshift = jnp.zeros((batch, RW_PROJ_W), F32)
        zwkv = jnp.zeros((batch, RW_HEADS, RW_HEAD, RW_HEAD), F32)
        yp, kp, vp, wp, sp = _layer(yp, zshift, zwkv, attend_p, p, batch, seq)
        ys, ks, vs, ws, ss = _layer(ys, state_shift[l], state_wkv[l], attend_s, p, db, dseq)
        kv_p = lambda a: a.reshape(batch, seq, ATT_HEADS, ATT_V_DIM)
        kv_s = lambda a: a.reshape(db, dseq, ATT_HEADS, ATT_V_DIM)
        for lst, val in zip(outs, (kv_p(kp), kv_p(vp), wp, sp, kv_s(ks), kv_s(vs), ws, ss)):
            lst.append(val)
    stacked = [jnp.stack(o) for o in outs]
    return (yp.reshape(batch, seq, D_MODEL), ys.reshape(db, dseq, D_MODEL), *stacked)
```

```python
import functools
import math

import jax
import jax.numpy as jnp
from jax import lax
from jax.experimental import pallas as pl
from jax.experimental.pallas import tpu as pltpu

F32 = jnp.float32
BF16 = jnp.bfloat16

D_MODEL = 1024
ATT_W = 512
RW_W = 512
ATT_HEADS = 4
ATT_QK_DIM = 64
ATT_V_DIM = 128
RW_HEAD = 64
RW_HEADS = 8
RW_PAIRS = RW_HEADS // 2
LORA_OFF = 3 * RW_W
GATE_OFF = LORA_OFF + 128
RW_PROJ_W = 1792
IN_W = 3 * ATT_W + RW_PROJ_W
D_FF = 2816
RMS_EPS = 1e-6
SUBLN_EPS = 1e-5
GN_EPS = 64e-5
NEG_BIG = -1e30

LANES = 128
VMEM_LIMIT = 56 * 1024 * 1024

ROW_TILE = 512
FFN_TM = 1024
FFN_TFF = 256
WKV_CHUNK = 64
WKV_SEQS = 4
INV_BASE = 8


def _cparams(sem):
    return pltpu.CompilerParams(dimension_semantics=sem, vmem_limit_bytes=VMEM_LIMIT)


def _full(shape):
    return pl.BlockSpec(shape, lambda *_: (0,) * len(shape))


def _mm(a, b):
    return jnp.dot(a.astype(BF16), b.astype(BF16), preferred_element_type=F32)


def _mm_nt(a, b):
    return lax.dot_general(a.astype(BF16), b.astype(BF16), (((1,), (1,)), ((), ())),
                           preferred_element_type=F32)


def _split2(x):
    hi = x.astype(BF16)
    lo = (x - hi.astype(F32)).astype(BF16)
    return hi, lo


def _split3(x):
    hi = x.astype(BF16)
    r1 = x - hi.astype(F32)
    mid = r1.astype(BF16)
    lo = (r1 - mid.astype(F32)).astype(BF16)
    return hi, mid, lo


def _seg_sum(a, seg_bf16):
    hi, lo = _split2(a)
    d = functools.partial(jnp.dot, preferred_element_type=F32)
    return d(hi, seg_bf16) + d(lo, seg_bf16)


def _const(shape):
    return pl.BlockSpec(shape, lambda *_: (0,) * len(shape), pipeline_mode=pl.Buffered(1))


def _mm_exact_lhs(a_bf16, b):
    h1, h2, h3 = _split3(b)
    d = functools.partial(jnp.dot, preferred_element_type=F32)
    return d(a_bf16, h1) + d(a_bf16, h2) + d(a_bf16, h3)


def _rms(x, g, eps):
    return x * lax.rsqrt(jnp.mean(x * x, axis=-1, keepdims=True) + eps) * g


def _diff_lambda(lq1, lk1, lq2, lk2, lam_init):
    s1 = jnp.sum(lq1[...] * lk1[...], axis=-1, keepdims=True)
    s2 = jnp.sum(lq2[...] * lk2[...], axis=-1, keepdims=True)
    return jnp.exp(s1) - jnp.exp(s2) + lam_init


def _softplus(x):
    return jnp.maximum(x, 0.0) + jnp.log1p(jnp.exp(-jnp.abs(x)))


def _sigmoid(x):
    return 1.0 / (1.0 + jnp.exp(-x))


def _inproj_kernel(x_ref, first_ref, g_ref, w_ref, mu_ref, w0_ref, a0_ref, kkw_ref, ka_ref, rk_ref,
                   w2_ref, a2_ref, g2_ref, seg_ref,
                   q_ref, k_ref, v_ref, k4_ref, v4_ref,
                   r_ref, lw_ref, k2_ref, vr_ref, kap_ref, b_ref, gate_ref, bonus_ref, tail_ref,
                   last_sc, *, period, tiles_per_seq):
    i = pl.program_id(0)
    h = _rms(x_ref[...], g_ref[...], RMS_EPS).astype(BF16)
    tm = h.shape[0]
    dot = functools.partial(jnp.dot, preferred_element_type=F32)
    q_ref[...] = dot(h, w_ref[:, 0:ATT_W])
    k = dot(h, w_ref[:, ATT_W:2 * ATT_W])
    v = dot(h, w_ref[:, 2 * ATT_W:3 * ATT_W])
    k_ref[...] = k
    v_ref[...] = v
    for hd in range(ATT_HEADS):
        sl = slice(hd * LANES, (hd + 1) * LANES)
        k4_ref[pl.ds(hd, tm, stride=ATT_HEADS), :] = k[:, sl]
        v4_ref[pl.ds(hd, tm, stride=ATT_HEADS), :] = v[:, sl]

    x = dot(h, w_ref[:, 3 * ATT_W:])
    row = lax.broadcasted_iota(jnp.int32, x.shape, 0)
    prev = pltpu.roll(x, 1, axis=0)
    if first_ref.shape[1] == 1:
        @pl.when(i == 0)
        def _():
            last_sc[...] = jnp.zeros(last_sc.shape, F32)
        lead = jnp.where(i % tiles_per_seq == 0, first_ref[0], last_sc[...])
        prev = jnp.where(row == 0, lead, prev)
        last_sc[...] = x[tm - 1:tm, :]
        tail_ref[0] = x[tm - 1:tm, :]
    else:
        prev = jnp.where(row % period == 0, first_ref[0], prev)
        tail_ref[...] = x
    xs = x + (prev - x) * mu_ref[...]
    r = xs[:, 0:RW_W]
    kx = xs[:, RW_W:2 * RW_W]
    vx = xs[:, 2 * RW_W:3 * RW_W]
    lo = xs[:, LORA_OFF:LORA_OFF + 128]
    g_lo = xs[:, GATE_OFF:GATE_OFF + 128]
    w_raw = -_softplus(-(w0_ref[...] + _mm(jnp.tanh(lo), w2_ref[...]))) - 0.5
    lw_ref[...] = -jnp.exp(w_raw)
    a = _sigmoid(a0_ref[...] + _mm(lo, a2_ref[...]))
    gate_ref[...] = _mm(_sigmoid(g_lo), g2_ref[...])
    kk = kx * kkw_ref[...]
    norm = jnp.sqrt(_seg_sum(kk * kk, seg_ref[...]))
    kap = kk / jnp.maximum(norm, 1e-12)
    k2 = kx * (1.0 + (a - 1.0) * ka_ref[...])
    r_ref[...] = r
    k2_ref[...] = k2
    vr_ref[...] = vx
    kap_ref[...] = kap
    b_ref[...] = kap * a
    bonus_ref[...] = _seg_sum(r * k2 * rk_ref[...], seg_ref[...]) * vx


def _inproj(x, first, period, tiles_per_seq, p, tm):
    m = x.shape[0]
    row = lambda w: pl.BlockSpec((tm, w), lambda i: (i, 0))
    heads_row = pl.BlockSpec((tm * ATT_HEADS, LANES), lambda i: (i, 0))
    vec = _const((1, RW_W))
    first_spec = pl.BlockSpec((1,) + first.shape[1:], lambda i: (i, 0, 0))
    wide = jax.ShapeDtypeStruct((m, RW_W), F32)
    if first.shape[1] == 1:
        tail_spec = pl.BlockSpec((1, 1, RW_PROJ_W), lambda i: (i, 0, 0))
        tail_shape = jax.ShapeDtypeStruct((m // tm, 1, RW_PROJ_W), F32)
    else:
        tail_spec = row(RW_PROJ_W)
        tail_shape = jax.ShapeDtypeStruct((m, RW_PROJ_W), F32)
    return pl.pallas_call(
        functools.partial(_inproj_kernel, period=period, tiles_per_seq=tiles_per_seq),
        grid=(m // tm,),
        in_specs=[row(D_MODEL), first_spec, _const((1, D_MODEL)), _const((D_MODEL, IN_W)),
                  _const((1, RW_PROJ_W)), vec, vec, vec, vec, vec,
                  _const((128, RW_W)), _const((128, RW_W)), _const((128, RW_W)), _const((RW_W, RW_W))],
        out_specs=[row(ATT_W)] * 3 + [heads_row] * 2 + [row(RW_W)] * 8 + [tail_spec],
        out_shape=[wide] * 3 + [jax.ShapeDtypeStruct((m * ATT_HEADS, LANES), F32)] * 2 + [wide] * 8
        + [tail_shape],
        scratch_shapes=[pltpu.VMEM((1, RW_PROJ_W), F32)],
        compiler_params=_cparams(("arbitrary",)),
        name="inproj",
    )(x, first, p['norm_mix_pre'], p['w_in'], p['mu'], p['w0'], p['a0'], p['kk'], p['ka'], p['rk'],
      p['w2pad'], p['a2pad'], p['g2'], p['seg'])


ATT_GRP = 128
ATT_ONES = 16
LOG2E = 1.4426950408889634


def _attn_prompt_kernel(lq1, lk1, lq2, lk2, sw_ref, q_ref, k_ref, v_ref, o_ref, kb_sc, vt_sc,
                        *, tq, lam_init):
    i = pl.program_id(2)
    groups = tq // ATT_GRP
    n_blk = k_ref.shape[0] // tq

    @pl.when(i == 0)
    def _():
        kb_sc[...] = k_ref[...].astype(BF16)
        ones = jnp.ones((ATT_ONES, tq), BF16)
        for jb in range(n_blk):
            vt_sc[jb] = jnp.concatenate([v_ref[jb * tq:(jb + 1) * tq, :].T.astype(BF16), ones], axis=0)

    lam = _diff_lambda(lq1, lk1, lq2, lk2, lam_init)
    lane = lax.broadcasted_iota(jnp.int32, (ATT_GRP, LANES), 1)
    q_maps = []
    for g in range(groups):
        q = q_ref[g * ATT_GRP:(g + 1) * ATT_GRP, :] * (ATT_QK_DIM ** -0.5 * LOG2E)
        q_maps.append(jnp.concatenate([jnp.where(lane < ATT_QK_DIM, q, 0.0),
                                       jnp.where(lane >= ATT_QK_DIM, q, 0.0)], axis=0).astype(BF16))

    def update(ss, vts, carry):
        ms, accs = carry
        m_new = _each(lambda m, s: jnp.maximum(m, jnp.max(s, axis=0, keepdims=True)), ms, ss)
        alpha = _each(lambda m, mn: jnp.exp2(m - mn), ms, m_new)
        ps = _each(lambda s, mn: jnp.exp2(s - mn).astype(BF16), ss, m_new)
        pv = _each(lambda vt, p: jnp.dot(vt, p, preferred_element_type=F32), vts, ps)
        accs = _each(lambda a, acc, x: a * acc + x, alpha, accs, pv)
        return tuple(m_new), tuple(accs)

    def full_block(j, carry):
        kb = kb_sc[pl.ds(pl.multiple_of(j * tq, tq), tq), :]
        vt = vt_sc[j]
        ss = [lax.dot_general(kb, qm, (((1,), (1,)), ((), ())), preferred_element_type=F32)
              for qm in q_maps]
        return update(ss, [vt] * len(q_maps), carry)

    w2 = 2 * ATT_GRP
    init = ((jnp.full((1, w2), NEG_BIG, F32),) * groups,
            (jnp.zeros((ATT_V_DIM + ATT_ONES, w2), F32),) * groups)
    carry = lax.fori_loop(0, i, full_block, init)

    start = pl.multiple_of(i * tq, tq)
    vt = vt_sc[i]
    ss, vts = [], []
    for g, qm in enumerate(q_maps):
        keys = (g + 1) * ATT_GRP
        kb = kb_sc[pl.ds(start, keys), :]
        s = lax.dot_general(kb, qm, (((1,), (1,)), ((), ())), preferred_element_type=F32)
        key = lax.broadcasted_iota(jnp.int32, (keys, w2), 0)
        qry = lax.broadcasted_iota(jnp.int32, (keys, w2), 1) % ATT_GRP + (keys - ATT_GRP)
        ss.append(jnp.where(key <= qry, s, -jnp.inf))
        vts.append(vt[:, :keys])
    _, accs = update(ss, vts, carry)
    for g in range(groups):
        z = accs[g][:ATT_V_DIM] / accs[g][ATT_V_DIM:ATT_V_DIM + 1]
        o = (z[:, :ATT_GRP] - lam * z[:, ATT_GRP:]).T
        o_ref[g * ATT_GRP:(g + 1) * ATT_GRP, :] = _rms(o, sw_ref[...], SUBLN_EPS) * (1.0 - lam_init)


def _attn_prompt(q, k, v, lam_params, subln_w, batch, seq, lam_init, tq=512):
    tq = min(tq, seq)
    nq = seq // tq
    lam_specs = [_full((1, ATT_QK_DIM))] * 4
    qspec = pl.BlockSpec((tq, LANES), lambda b, h, i: (b * nq + i, h))
    kvspec = pl.BlockSpec((seq, LANES), lambda b, h, i: (b, h))
    return pl.pallas_call(
        functools.partial(_attn_prompt_kernel, tq=tq, lam_init=lam_init),
        grid=(batch, ATT_HEADS, nq),
        in_specs=lam_specs + [_full((1, ATT_V_DIM)), qspec, kvspec, kvspec],
        out_specs=qspec,
        out_shape=jax.ShapeDtypeStruct((batch * seq, ATT_W), F32),
        scratch_shapes=[pltpu.VMEM((seq, LANES), BF16),
                        pltpu.VMEM((nq, ATT_V_DIM + ATT_ONES, tq), BF16)],
        compiler_params=_cparams(("parallel", "parallel", "arbitrary")),
        name="attn_prompt",
    )(*lam_params, subln_w, q, k, v)


def _attn_paged_kernel(pt_ref, lq1, lk1, lq2, lk2, sw_ref, q_ref, kn_ref, vn_ref, *rest,
                       pages_per_step, n_tok, lam_init):
    del pt_ref
    k_refs = rest[:pages_per_step]
    v_refs = rest[pages_per_step:2 * pages_per_step]
    o_ref, m_sc, l_sc, acc_sc = rest[2 * pages_per_step:]
    c = pl.program_id(1)
    half = n_tok * ATT_HEADS
    rows = 2 * half
    page_rows = k_refs[0].shape[1]

    @pl.when(c == 0)
    def _():
        m_sc[...] = jnp.full(m_sc.shape, NEG_BIG, F32)
        l_sc[...] = jnp.zeros(l_sc.shape, F32)
        acc_sc[...] = jnp.zeros(acc_sc.shape, F32)

    row = lax.broadcasted_iota(jnp.int32, (rows, LANES), 0)
    lane = lax.broadcasted_iota(jnp.int32, (rows, LANES), 1)
    qs = jnp.where((lane // ATT_QK_DIM) == (row // half), q_ref[0], 0.0) * (ATT_QK_DIM ** -0.5)
    qs_b = qs.astype(BF16)
    srow = lax.broadcasted_iota(jnp.int32, (rows, page_rows), 0)
    scol = lax.broadcasted_iota(jnp.int32, (rows, page_rows), 1)
    same_head = (scol % ATT_HEADS) == (srow % ATT_HEADS)

    ss = [jnp.where(same_head, _mm_nt(qs_b, kr[0]), -jnp.inf) for kr in k_refs]
    m_old, l_old, acc_old = m_sc[...], l_sc[...], acc_sc[...]
    m = jnp.maximum(m_old, jnp.max(functools.reduce(jnp.maximum, ss), axis=-1, keepdims=True))
    alpha = jnp.exp(m_old - m)
    ps = [jnp.exp(s - m) for s in ss]
    l = alpha * l_old + jnp.sum(functools.reduce(jnp.add, ps), axis=-1, keepdims=True)
    pv = functools.reduce(jnp.add, [_mm(p, vr[0]) for p, vr in zip(ps, v_refs)])
    acc = alpha * acc_old + pv
    m_sc[...], l_sc[...], acc_sc[...] = m, l, acc

    @pl.when(c == pl.num_programs(1) - 1)
    def _():
        lam = _diff_lambda(lq1, lk1, lq2, lk2, lam_init)
        mm, ll, aa = m, l, acc
        r1 = lax.broadcasted_iota(jnp.int32, (rows, 1), 0)
        tok = (r1 // ATT_HEADS) % n_tok
        for j in range(n_tok):
            s = jnp.sum(qs * kn_ref[0, j], axis=-1, keepdims=True)
            s = jnp.where(tok >= j, s, -jnp.inf)
            m_new = jnp.maximum(mm, s)
            alpha = jnp.exp(mm - m_new)
            p = jnp.exp(s - m_new)
            ll = alpha * ll + p
            aa = alpha * aa + p * vn_ref[0, j]
            mm = m_new
        z = aa / ll
        o = z[:half] - lam * z[half:]
        o_ref[0] = _rms(o, sw_ref[...], SUBLN_EPS) * (1.0 - lam_init)


def _attn_paged(q, kn, vn, cache_k, cache_v, page_table, lam_params, subln_w, lam_init,
                pages_per_step=32):
    db, n_tok, _ = q.shape
    n_pages = page_table.shape[1]
    page_rows = cache_k.shape[1]
    pages_per_step = math.gcd(pages_per_step, n_pages)
    steps = n_pages // pages_per_step
    half = n_tok * ATT_HEADS
    rows = 2 * half
    pt_flat = page_table.reshape(-1)
    heads = lambda a: a.reshape(db, n_tok, ATT_HEADS, LANES)
    q_rows = jnp.broadcast_to(heads(q)[:, None], (db, 2, n_tok, ATT_HEADS, LANES)).reshape(db, rows, LANES)
    rep = lambda a: jnp.broadcast_to(
        heads(a)[:, :, None, None], (db, n_tok, 2, n_tok, ATT_HEADS, LANES)).reshape(db, n_tok, rows, LANES)

    def page_spec(i):
        return pl.BlockSpec(
            (1, page_rows, LANES),
            lambda b, c, pt: (pt[b * n_pages + c * pages_per_step + i], 0, 0))

    const = lambda shape: pl.BlockSpec(shape, lambda b, c, pt: (0,) * len(shape))
    new_spec = pl.BlockSpec((1, n_tok, rows, LANES), lambda b, c, pt: (b, 0, 0, 0))
    grid_spec = pltpu.PrefetchScalarGridSpec(
        num_scalar_prefetch=1,
        grid=(db, steps),
        in_specs=[const((1, ATT_QK_DIM))] * 4 + [const((1, ATT_V_DIM))]
        + [pl.BlockSpec((1, rows, LANES), lambda b, c, pt: (b, 0, 0)), new_spec, new_spec]
        + [page_spec(i) for i in range(pages_per_step)] * 2,
        out_specs=pl.BlockSpec((1, half, LANES), lambda b, c, pt: (b, 0, 0)),
        scratch_shapes=[pltpu.VMEM((rows, 1), F32), pltpu.VMEM((rows, 1), F32),
                        pltpu.VMEM((rows, LANES), F32)],
    )
    out = pl.pallas_call(
        functools.partial(_attn_paged_kernel, pages_per_step=pages_per_step, n_tok=n_tok,
                          lam_init=lam_init),
        grid_spec=grid_spec,
        out_shape=jax.ShapeDtypeStruct((db, half, LANES), F32),
        compiler_params=_cparams(("parallel", "arbitrary")),
        name="attn_paged",
    )(pt_flat, *lam_params, subln_w, q_rows, rep(kn), rep(vn),
      *([cache_k] * pages_per_step), *([cache_v] * pages_per_step))
    return out.reshape(db, n_tok, ATT_W)


def _stack(x):
    lane = lax.broadcasted_iota(jnp.int32, x.shape, 1)
    return jnp.concatenate([jnp.where(lane < RW_HEAD, x, 0.0),
                            jnp.where(lane >= RW_HEAD, x, 0.0)], axis=0)


def _each(fn, *lists):
    return [fn(*xs) for xs in zip(*lists)]


def _unit_lower_inverses(lmats, c):
    n = 2 * c
    ri = lax.broadcasted_iota(jnp.int32, (n, n), 0)
    ci = lax.broadcasted_iota(jnp.int32, (n, n), 1)
    eye = jnp.where(ri == ci, 1.0, 0.0)
    same = lambda s: (ri // s) == (ci // s)
    m1 = _each(lambda l: jnp.where(same(INV_BASE), -l, 0.0), lmats)
    m2 = _each(_mm, m1, m1)
    m4 = _each(_mm, m2, m2)
    t = _each(lambda a, b: eye + a + b + _mm(a, b), m1, m2)
    t = _each(lambda a, b: a + _mm(a, b), t, m4)
    s = INV_BASE
    while s < c:
        keep = same(2 * s) & jnp.logical_not(same(s))
        x = _each(lambda l, a: _mm(jnp.where(keep, l, 0.0), a), lmats, t)
        t = _each(lambda a, b: a - _mm(a, b), t, x)
        s *= 2
    return t


def _wkv_scan_kernel(r_ref, lw_ref, k_ref, v_ref, kap_ref, b_ref, s0_ref, tri_ref,
                     y_ref, sout_ref, st_sc, *, c, nseq):
    ci = pl.program_id(1)
    n = 2 * c
    problems = [(g, p) for g in range(nseq) for p in range(RW_PAIRS)]

    @pl.when(ci == 0)
    def _():
        zero = jnp.zeros((RW_HEAD, RW_HEAD), F32)
        for i, (g, p) in enumerate(problems):
            top = jnp.concatenate([s0_ref[g, 2 * p], zero], axis=1)
            bot = jnp.concatenate([zero, s0_ref[g, 2 * p + 1]], axis=1)
            st_sc[i] = jnp.concatenate([top, bot], axis=0)

    ri = lax.broadcasted_iota(jnp.int32, (n, n), 0)
    cj = lax.broadcasted_iota(jnp.int32, (n, n), 1)
    same_head = (ri // c) == (cj // c)
    strict = same_head & (ri > cj)
    incl = same_head & (ri >= cj)
    tri = tri_ref[...]

    sls = [slice(p * LANES, (p + 1) * LANES) for p in range(RW_PAIRS)]
    load = lambda ref: [ref[g, 0, :, sls[p]] for g, p in problems]
    lw, kap, r, k, b, v = (load(ref) for ref in (lw_ref, kap_ref, r_ref, k_ref, b_ref, v_ref))
    dup = lambda x: jnp.concatenate([x, x], axis=0)
    cs = _each(lambda x: _mm_exact_lhs(tri, x), lw)
    cs_end = [x[c - 1:c, :] for x in cs]
    kh = _each(lambda a, x, w: a * jnp.exp(x - w), kap, cs, lw)
    rt = _each(lambda a, x: a * jnp.exp(x), r, cs)
    e_out = [jnp.exp(-x) for x in cs]
    kt = _each(jnp.multiply, k, e_out)
    bt = _each(jnp.multiply, b, e_out)
    e_end = _each(lambda xe, x: jnp.exp(xe - x), cs_end, cs)
    k_end = _each(jnp.multiply, k, e_end)
    b_end = _each(jnp.multiply, b, e_end)

    khs, rts = _each(_stack, kh), _each(_stack, rt)
    bb, kk = _each(dup, bt), _each(dup, kt)
    cat = lambda x, y: jnp.concatenate([x, y], axis=0)
    if n % LANES == 0:
        prod = _each(lambda x1, x2, y1, y2: _mm_nt(cat(x1, x2), cat(y1, y2)), khs, rts, bb, kk)
        quad = lambda i, j: [x[i * n:(i + 1) * n, j * n:(j + 1) * n] for x in prod]
        p_kb, p_kk, p_rb, p_rk = quad(0, 0), quad(0, 1), quad(1, 0), quad(1, 1)
    else:
        p_kb, p_kk = _each(_mm_nt, khs, bb), _each(_mm_nt, khs, kk)
        p_rb, p_rk = _each(_mm_nt, rts, bb), _each(_mm_nt, rts, kk)
    lmat = [jnp.where(strict, x, 0.0) for x in p_kb]
    a_kk = [jnp.where(strict, x, 0.0) for x in p_kk]
    a_rb = [jnp.where(incl, x, 0.0) for x in p_rb]
    a_rk = [jnp.where(incl, x, 0.0) for x in p_rk]
    st = [st_sc[i] for i in range(len(problems))]
    vs = _each(_stack, v)
    from_state = _each(lambda x, y, s: _mm_nt(cat(x, y), s), kh, rt, st)
    from_v = _each(lambda a1, a2, y: _mm(cat(a1, a2), y), a_kk, a_rk, vs)
    rhs = _each(lambda s, a: _stack(s[:c]) + a[:n], from_state, from_v)
    y0 = _each(lambda s, a: _stack(s[c:]) + a[n:], from_state, from_v)
    tinv = _unit_lower_inverses(lmat, c)
    u = _each(_mm, tinv, rhs)
    ys = _each(lambda y, a, x: y - _mm(a, x), y0, a_rb, u)
    vu = _each(lambda x, y: jnp.concatenate([x, y], axis=0), vs, u)
    kb = _each(lambda x, y: jnp.concatenate([_stack(x), -_stack(y)], axis=0), k_end, b_end)
    st_new = _each(lambda s, xe, x, y: s * jnp.exp(xe) + _mm(x.T, y), st, cs_end, vu, kb)
    for i, (g, p) in enumerate(problems):
        y_ref[g, 0, :, sls[p]] = ys[i][:c] + ys[i][c:]
        st_sc[i] = st_new[i]

    @pl.when(ci == pl.num_programs(1) - 1)
    def _():
        for i, (g, p) in enumerate(problems):
            st = st_sc[i]
            sout_ref[g, 2 * p] = st[:RW_HEAD, :RW_HEAD]
            sout_ref[g, 2 * p + 1] = st[RW_HEAD:, RW_HEAD:]


def _wkv_scan(r, lw, k, v, kap, b, s0, batch, seq, c, nseq=WKV_SEQS):
    nc = seq // c
    nseq = math.gcd(nseq, batch)
    tile = pl.BlockSpec((nseq, 1, c, RW_W), lambda bi, ci: (bi, ci, 0, 0))
    sspec = pl.BlockSpec((nseq, RW_HEADS, RW_HEAD, RW_HEAD), lambda bi, ci: (bi, 0, 0, 0))
    tri = jnp.tril(jnp.ones((c, c), F32)).astype(BF16)
    tiles = [t.reshape(batch, nc, c, RW_W) for t in (r, lw, k, v, kap, b)]
    y, s_new = pl.pallas_call(
        functools.partial(_wkv_scan_kernel, c=c, nseq=nseq),
        grid=(batch // nseq, nc),
        in_specs=[tile] * 6 + [sspec, _full((c, c))],
        out_specs=[tile, sspec],
        out_shape=[jax.ShapeDtypeStruct((batch, nc, c, RW_W), F32),
                   jax.ShapeDtypeStruct((batch, RW_HEADS, RW_HEAD, RW_HEAD), F32)],
        scratch_shapes=[pltpu.VMEM((nseq * RW_PAIRS, LANES, LANES), F32)],
        compiler_params=_cparams(("parallel", "arbitrary")),
        name="wkv_scan",
    )(*tiles, s0, tri)
    return y.reshape(batch * seq, RW_W), s_new


def _mix_out_kernel(x_ref, att_ref, y_ref, bonus_ref, g_ref, lnw_ref, lnb_ref, seg_ref,
                    woa_ref, wor_ref, gpost_ref, o_ref):
    y = y_ref[...]
    seg = seg_ref[...]
    mean = _seg_sum(y, seg) * (1.0 / RW_HEAD)
    d = y - mean
    var = _seg_sum(d * d, seg) * (1.0 / RW_HEAD)
    yn = d * lax.rsqrt(var + GN_EPS) * lnw_ref[...] + lnb_ref[...]
    rw_out = (yn + bonus_ref[...]) * g_ref[...]
    mix = _mm(att_ref[...], woa_ref[...]) + _mm(rw_out, wor_ref[...])
    o_ref[...] = x_ref[...] + _rms(mix, gpost_ref[...], RMS_EPS)


def _mix_out(x, att, y, bonus, g, p, tm):
    m = x.shape[0]
    row = lambda w: pl.BlockSpec((tm, w), lambda i: (i, 0))
    vec = _full((1, RW_W))
    return pl.pallas_call(
        _mix_out_kernel,
        grid=(m // tm,),
        in_specs=[row(D_MODEL), row(ATT_W), row(RW_W), row(RW_W), row(RW_W), vec, vec,
                  _full((RW_W, RW_W)), _full((ATT_W, D_MODEL)), _full((RW_W, D_MODEL)),
                  _full((1, D_MODEL))],
        out_specs=row(D_MODEL),
        out_shape=jax.ShapeDtypeStruct((m, D_MODEL), F32),
        compiler_params=_cparams(("parallel",)),
        name="mix_out",
    )(x, att, y, bonus, g, p['lnx_w'], p['lnx_b'], p['seg'], p['w_out_att'], p['w_out_rw'],
      p['norm_mix_post'])


def _ffn_kernel(x_ref, gpre_ref, gpost_ref, wg_ref, wu_ref, wd_ref, o_ref, f_sc, acc_sc):
    j = pl.program_id(1)

    @pl.when(j == 0)
    def _():
        f_sc[...] = _rms(x_ref[...], gpre_ref[...], RMS_EPS).astype(BF16)
        acc_sc[...] = jnp.zeros(acc_sc.shape, F32)

    f = f_sc[...]
    gate = jnp.dot(f, wg_ref[...], preferred_element_type=F32)
    up = jnp.dot(f, wu_ref[...], preferred_element_type=F32)
    h = gate * _sigmoid(gate) * up
    acc_sc[...] += jnp.dot(h.astype(BF16), wd_ref[...], preferred_element_type=F32)

    @pl.when(j == pl.num_programs(1) - 1)
    def _():
        o_ref[...] = x_ref[...] + _rms(acc_sc[...], gpost_ref[...], RMS_EPS)


def _ffn(x, p, tm, tff):
    m = x.shape[0]
    row = pl.BlockSpec((tm, D_MODEL), lambda i, j: (i, 0))
    vec = pl.BlockSpec((1, D_MODEL), lambda i, j: (0, 0))
    return pl.pallas_call(
        _ffn_kernel,
        grid=(m // tm, D_FF // tff),
        in_specs=[row, vec, vec,
                  pl.BlockSpec((D_MODEL, tff), lambda i, j: (0, j)),
                  pl.BlockSpec((D_MODEL, tff), lambda i, j: (0, j)),
                  pl.BlockSpec((tff, D_MODEL), lambda i, j: (j, 0))],
        out_specs=row,
        out_shape=jax.ShapeDtypeStruct((m, D_MODEL), F32),
        scratch_shapes=[pltpu.VMEM((tm, D_MODEL), BF16), pltpu.VMEM((tm, D_MODEL), F32)],
        compiler_params=_cparams(("parallel", "arbitrary")),
        name="ffn",
    )(x, p['norm_ffn_pre'], p['norm_ffn_post'], p['ffn_gate'], p['ffn_up'], p['ffn_down'])


def _layer_params(l, w):
    row = lambda a: a[l].reshape(1, -1).astype(F32)
    zeros64 = jnp.zeros((64, RW_W), F32)
    head = jnp.arange(RW_W) // RW_HEAD
    return {
        'norm_mix_pre': row(w['norm_mix_pre']), 'norm_mix_post': row(w['norm_mix_post']),
        'norm_ffn_pre': row(w['norm_ffn_pre']), 'norm_ffn_post': row(w['norm_ffn_post']),
        'w_in': w['w_in'][l].astype(BF16),
        'w_out_att': w['w_out'][l][:ATT_W].astype(BF16),
        'w_out_rw': w['w_out'][l][ATT_W:].astype(BF16),
        'lam': [row(w[n]) for n in ('lambda_q1', 'lambda_k1', 'lambda_q2', 'lambda_k2')],
        'subln_w': row(w['subln_w']),
        'mu': row(w['rw_mu']), 'w0': row(w['rw_w0']), 'a0': row(w['rw_a0']),
        'kk': row(w['rw_kk']), 'ka': row(w['rw_ka']), 'rk': row(w['rw_rk']),
        'lnx_w': row(w['rw_lnx_w']), 'lnx_b': row(w['rw_lnx_b']),
        'w2pad': jnp.concatenate([w['rw_w2'][l], zeros64], axis=0).astype(BF16),
        'a2pad': jnp.concatenate([zeros64, w['rw_a2'][l]], axis=0).astype(BF16),
        'g2': w['rw_g2'][l].astype(BF16),
        'seg': (head[:, None] == head[None, :]).astype(BF16),
        'ffn_gate': w['ffn_gate'][l].astype(BF16), 'ffn_up': w['ffn_up'][l].astype(BF16),
        'ffn_down': w['ffn_down'][l].astype(BF16),
    }


def _layer(x, shift0, wkv0, attend, p, batch, seq):
    m = batch * seq
    if seq >= WKV_CHUNK:
        tm = min(ROW_TILE, seq)
        tiles = seq // tm
        first = jnp.zeros((batch, tiles, RW_PROJ_W), F32).at[:, 0, :].set(shift0).reshape(m // tm, 1, RW_PROJ_W)
        period, c = tm, WKV_CHUNK
    else:
        tm = min(ROW_TILE, m)
        tiles = 1
        first = jnp.zeros((batch, seq, RW_PROJ_W), F32).at[:, 0, :].set(shift0).reshape(m // tm, tm, RW_PROJ_W)
        period, c = seq, 8
    q, k, v, k4, v4, r, lw, k2, vr, kap, b, gate, bonus, tail = _inproj(x, first, period, tiles, p, tm)
    att = attend(q, k, v)
    scan_in = (r, lw, k2, vr, kap, b)
    if c > seq:
        pad = lambda t: jnp.pad(t.reshape(batch, seq, RW_W), ((0, 0), (0, c - seq), (0, 0))).reshape(batch * c, RW_W)
        y, wkv_new = _wkv_scan(*[pad(t) for t in scan_in], wkv0, batch, c, c)
        y = y.reshape(batch, c, RW_W)[:, :seq].reshape(m, RW_W)
        shift_new = tail.reshape(batch, seq, RW_PROJ_W)[:, -1]
    else:
        y, wkv_new = _wkv_scan(*scan_in, wkv0, batch, seq, c)
        shift_new = tail.reshape(batch, tiles, RW_PROJ_W)[:, -1]
    x1 = _mix_out(x, att, y, bonus, gate, p, tm)
    out = _ffn(x1, p, min(FFN_TM, m), FFN_TFF)
    return out, k4, v4, wkv_new, shift_new


def kernel(x_prompt, x_sample, cache_k, cache_v, state_wkv, state_shift, page_table, norm_mix_pre, norm_mix_post, norm_ffn_pre, norm_ffn_post, w_in, w_out, lambda_q1, lambda_k1, lambda_q2, lambda_k2, subln_w, rw_mu, rw_w0, rw_w2, rw_a0, rw_a2, rw_g2, rw_kk, rw_ka, rw_rk, rw_lnx_w, rw_lnx_b, ffn_gate, ffn_up, ffn_down):
    weights = dict(
        norm_mix_pre=norm_mix_pre, norm_mix_post=norm_mix_post, norm_ffn_pre=norm_ffn_pre,
        norm_ffn_post=norm_ffn_post, w_in=w_in, w_out=w_out, lambda_q1=lambda_q1,
        lambda_k1=lambda_k1, lambda_q2=lambda_q2, lambda_k2=lambda_k2, subln_w=subln_w,
        rw_mu=rw_mu, rw_w0=rw_w0, rw_w2=rw_w2, rw_a0=rw_a0, rw_a2=rw_a2, rw_g2=rw_g2,
        rw_kk=rw_kk, rw_ka=rw_ka, rw_rk=rw_rk, rw_lnx_w=rw_lnx_w, rw_lnx_b=rw_lnx_b,
        ffn_gate=ffn_gate, ffn_up=ffn_up, ffn_down=ffn_down)
    depth = w_in.shape[0]
    batch, seq, _ = x_prompt.shape
    db, dseq, _ = x_sample.shape
    n_pool, page = cache_k.shape[1], cache_k.shape[2]
    yp = x_prompt.reshape(batch * seq, D_MODEL)
    ys = x_sample.reshape(db * dseq, D_MODEL)
    outs = [[] for _ in range(8)]
    for l in range(depth):
        lam_init = 0.8 - 0.6 * math.exp(-0.3 * l)
        p = _layer_params(l, weights)
        ck = cache_k[l].reshape(n_pool, page * ATT_HEADS, LANES)
        cv = cache_v[l].reshape(n_pool, page * ATT_HEADS, LANES)

        def attend_p(q, k, v):
            return _attn_prompt(q, k, v, p['lam'], p['subln_w'], batch, seq, lam_init)

        def attend_s(q, k, v):
            t3 = lambda a: a.reshape(db, dseq, ATT_W)
            o = _attn_paged(t3(q), t3(k), t3(v), ck, cv, page_table, p['lam'], p['subln_w'], lam_init)
            return o.reshape(db * dseq, ATT_W)

        zshift = jnp.zeros((batch, RW_PROJ_W), F32)
        zwkv = jnp.zeros((batch, RW_HEADS, RW_HEAD, RW_HEAD), F32)
        yp, kp, vp, wp, sp = _layer(yp, zshift, zwkv, attend_p, p, batch, seq)
        ys, ks, vs, ws, ss = _layer(ys, state_shift[l], state_wkv[l], attend_s, p, db, dseq)
        kv_p = lambda a: a.reshape(batch, seq, ATT_HEADS, ATT_V_DIM)
        kv_s = lambda a: a.reshape(db, dseq, ATT_HEADS, ATT_V_DIM)
        for lst, val in zip(outs, (kv_p(kp), kv_p(vp), wp, sp, kv_s(ks), kv_s(vs), ws, ss)):
            lst.append(val)
    stacked = [jnp.stack(o) for o in outs]
    return (yp.reshape(batch, seq, D_MODEL), ys.reshape(db, dseq, D_MODEL), *stacked)
```

```python
import functools
import math

import jax
import jax.numpy as jnp
from jax import lax
from jax.experimental import pallas as pl
from jax.experimental.pallas import tpu as pltpu

F32 = jnp.float32
BF16 = jnp.bfloat16

D_MODEL = 1024
ATT_W = 512
RW_W = 512
ATT_HEADS = 4
ATT_QK_DIM = 64
ATT_V_DIM = 128
RW_HEAD = 64
RW_HEADS = 8
RW_PAIRS = RW_HEADS // 2
LORA_OFF = 3 * RW_W
GATE_OFF = LORA_OFF + 128
RW_PROJ_W = 1792
IN_W = 3 * ATT_W + RW_PROJ_W
D_FF = 2816
RMS_EPS = 1e-6
SUBLN_EPS = 1e-5
GN_EPS = 64e-5
NEG_BIG = -1e30

LANES = 128
VMEM_LIMIT = 56 * 1024 * 1024

ROW_TILE = 512
FFN_CHUNKS = 2
WKV_CHUNK = 64
WKV_SEQS = 4
INV_BASE = 8


def _cparams(sem):
    return pltpu.CompilerParams(dimension_semantics=sem, vmem_limit_bytes=VMEM_LIMIT)


def _full(shape):
    return pl.BlockSpec(shape, lambda *_: (0,) * len(shape))


def _mm(a, b):
    return jnp.dot(a.astype(BF16), b.astype(BF16), preferred_element_type=F32)


def _mm_nt(a, b):
    return lax.dot_general(a.astype(BF16), b.astype(BF16), (((1,), (1,)), ((), ())),
                           preferred_element_type=F32)


def _split2(x):
    hi = x.astype(BF16)
    lo = (x - hi.astype(F32)).astype(BF16)
    return hi, lo


def _split3(x):
    hi = x.astype(BF16)
    r1 = x - hi.astype(F32)
    mid = r1.astype(BF16)
    lo = (r1 - mid.astype(F32)).astype(BF16)
    return hi, mid, lo


def _seg_sum(a, seg_bf16):
    hi, lo = _split2(a)
    d = functools.partial(jnp.dot, preferred_element_type=F32)
    return d(hi, seg_bf16) + d(lo, seg_bf16)


def _const(shape):
    return pl.BlockSpec(shape, lambda *_: (0,) * len(shape), pipeline_mode=pl.Buffered(1))


def _mm_exact_lhs(a_bf16, b):
    h1, h2, h3 = _split3(b)
    d = functools.partial(jnp.dot, preferred_element_type=F32)
    return d(a_bf16, h1) + d(a_bf16, h2) + d(a_bf16, h3)


def _rms(x, g, eps):
    return x * lax.rsqrt(jnp.mean(x * x, axis=-1, keepdims=True) + eps) * g


def _diff_lambda(lq1, lk1, lq2, lk2, lam_init):
    s1 = jnp.sum(lq1[...] * lk1[...], axis=-1, keepdims=True)
    s2 = jnp.sum(lq2[...] * lk2[...], axis=-1, keepdims=True)
    return jnp.exp(s1) - jnp.exp(s2) + lam_init


def _softplus(x):
    return jnp.maximum(x, 0.0) + jnp.log1p(jnp.exp(-jnp.abs(x)))


def _sigmoid(x):
    return 1.0 / (1.0 + jnp.exp(-x))


def _inproj_kernel(x_ref, first_ref, g_ref, w_ref, mu_ref, w0_ref, a0_ref, kkw_ref, ka_ref, rk_ref,
                   w2_ref, a2_ref, g2_ref, seg_ref,
                   q_ref, k_ref, v_ref, k4_ref, v4_ref,
                   r_ref, lw_ref, k2_ref, vr_ref, kap_ref, b_ref, gate_ref, bonus_ref, tail_ref,
                   last_sc, *, period, tiles_per_seq):
    i = pl.program_id(0)
    tm = x_ref.shape[0]
    dot = functools.partial(jnp.dot, preferred_element_type=F32)
    carried = first_ref.shape[1] == 1
    halves = [slice(0, tm // 2), slice(tm // 2, tm)]
    xh = []
    for rs in halves:
        n = rs.stop - rs.start
        h = _rms(x_ref[rs, :], g_ref[...], RMS_EPS).astype(BF16)
        q_ref[rs, :] = dot(h, w_ref[:, 0:ATT_W])
        k = dot(h, w_ref[:, ATT_W:2 * ATT_W])
        v = dot(h, w_ref[:, 2 * ATT_W:3 * ATT_W])
        k_ref[rs, :] = k
        v_ref[rs, :] = v
        for hd in range(ATT_HEADS):
            sl = slice(hd * LANES, (hd + 1) * LANES)
            k4_ref[pl.ds(rs.start * ATT_HEADS + hd, n, stride=ATT_HEADS), :] = k[:, sl]
            v4_ref[pl.ds(rs.start * ATT_HEADS + hd, n, stride=ATT_HEADS), :] = v[:, sl]
        xh.append(dot(h, w_ref[:, 3 * ATT_W:]))

    if carried:
        @pl.when(i == 0)
        def _():
            last_sc[...] = jnp.zeros(last_sc.shape, F32)
        lead = jnp.where(i % tiles_per_seq == 0, first_ref[0], last_sc[...])
        last_sc[...] = xh[-1][-1:, :]
        tail_ref[0] = xh[-1][-1:, :]

    for idx, rs in enumerate(halves):
        x = xh[idx]
        row = lax.broadcasted_iota(jnp.int32, x.shape, 0)
        prev = pltpu.roll(x, 1, axis=0)
        if carried:
            prev = jnp.where(row == 0, lead if idx == 0 else xh[idx - 1][-1:, :], prev)
        else:
            prev = jnp.where(row % period == 0, first_ref[0, rs, :], prev)
            tail_ref[rs, :] = x
        xs = x + (prev - x) * mu_ref[...]
        r = xs[:, 0:RW_W]
        kx = xs[:, RW_W:2 * RW_W]
        vx = xs[:, 2 * RW_W:3 * RW_W]
        lo = xs[:, LORA_OFF:LORA_OFF + 128]
        g_lo = xs[:, GATE_OFF:GATE_OFF + 128]
        w_raw = -_softplus(-(w0_ref[...] + _mm(jnp.tanh(lo), w2_ref[...]))) - 0.5
        lw_ref[rs, :] = -jnp.exp(w_raw)
        a = _sigmoid(a0_ref[...] + _mm(lo, a2_ref[...]))
        gate_ref[rs, :] = _mm(_sigmoid(g_lo), g2_ref[...])
        kk = kx * kkw_ref[...]
        norm = jnp.sqrt(_seg_sum(kk * kk, seg_ref[...]))
        kap = kk / jnp.maximum(norm, 1e-12)
        k2 = kx * (1.0 + (a - 1.0) * ka_ref[...])
        r_ref[rs, :] = r
        k2_ref[rs, :] = k2
        vr_ref[rs, :] = vx
        kap_ref[rs, :] = kap
        b_ref[rs, :] = kap * a
        bonus_ref[rs, :] = _seg_sum(r * k2 * rk_ref[...], seg_ref[...]) * vx


def _inproj(x, first, period, tiles_per_seq, p, tm):
    m = x.shape[0]
    row = lambda w: pl.BlockSpec((tm, w), lambda i: (i, 0))
    heads_row = pl.BlockSpec((tm * ATT_HEADS, LANES), lambda i: (i, 0))
    vec = _const((1, RW_W))
    first_spec = pl.BlockSpec((1,) + first.shape[1:], lambda i: (i, 0, 0))
    wide = jax.ShapeDtypeStruct((m, RW_W), F32)
    if first.shape[1] == 1:
        tail_spec = pl.BlockSpec((1, 1, RW_PROJ_W), lambda i: (i, 0, 0))
        tail_shape = jax.ShapeDtypeStruct((m // tm, 1, RW_PROJ_W), F32)
    else:
        tail_spec = row(RW_PROJ_W)
        tail_shape = jax.ShapeDtypeStruct((m, RW_PROJ_W), F32)
    return pl.pallas_call(
        functools.partial(_inproj_kernel, period=period, tiles_per_seq=tiles_per_seq),
        grid=(m // tm,),
        in_specs=[row(D_MODEL), first_spec, _const((1, D_MODEL)), _const((D_MODEL, IN_W)),
                  _const((1, RW_PROJ_W)), vec, vec, vec, vec, vec,
                  _const((128, RW_W)), _const((128, RW_W)), _const((128, RW_W)), _const((RW_W, RW_W))],
        out_specs=[row(ATT_W)] * 3 + [heads_row] * 2 + [row(RW_W)] * 8 + [tail_spec],
        out_shape=[wide] * 3 + [jax.ShapeDtypeStruct((m * ATT_HEADS, LANES), F32)] * 2 + [wide] * 8
        + [tail_shape],
        scratch_shapes=[pltpu.VMEM((1, RW_PROJ_W), F32)],
        compiler_params=_cparams(("arbitrary",)),
        name="inproj",
    )(x, first, p['norm_mix_pre'], p['w_in'], p['mu'], p['w0'], p['a0'], p['kk'], p['ka'], p['rk'],
      p['w2pad'], p['a2pad'], p['g2'], p['seg'])


ATT_GRP = 128
ATT_ONES = 16
LOG2E = 1.4426950408889634


def _attn_prompt_kernel(lq1, lk1, lq2, lk2, sw_ref, q_ref, k_ref, v_ref, o_ref, kb_sc, vt_sc,
                        *, tq, lam_init):
    i = pl.program_id(2)
    groups = tq // ATT_GRP
    n_blk = k_ref.shape[0] // tq

    @pl.when(i == 0)
    def _():
        kb_sc[...] = k_ref[...].astype(BF16)
        ones = jnp.ones((ATT_ONES, tq), BF16)
        for jb in range(n_blk):
            vt_sc[jb] = jnp.concatenate([v_ref[jb * tq:(jb + 1) * tq, :].T.astype(BF16), ones], axis=0)

    lam = _diff_lambda(lq1, lk1, lq2, lk2, lam_init)
    lane = lax.broadcasted_iota(jnp.int32, (ATT_GRP, LANES), 1)
    q_maps = []
    for g in range(groups):
        q = q_ref[g * ATT_GRP:(g + 1) * ATT_GRP, :] * (ATT_QK_DIM ** -0.5 * LOG2E)
        q_maps.append(jnp.concatenate([jnp.where(lane < ATT_QK_DIM, q, 0.0),
                                       jnp.where(lane >= ATT_QK_DIM, q, 0.0)], axis=0).astype(BF16))

    def update(ss, vts, carry):
        ms, accs = carry
        m_new = _each(lambda m, s: jnp.maximum(m, jnp.max(s, axis=0, keepdims=True)), ms, ss)
        alpha = _each(lambda m, mn: jnp.exp2(m - mn), ms, m_new)
        ps = _each(lambda s, mn: jnp.exp2(s - mn).astype(BF16), ss, m_new)
        pv = _each(lambda vt, p: jnp.dot(vt, p, preferred_element_type=F32), vts, ps)
        accs = _each(lambda a, acc, x: a * acc + x, alpha, accs, pv)
        return tuple(m_new), tuple(accs)

    def full_block(j, carry):
        kb = kb_sc[pl.ds(pl.multiple_of(j * tq, tq), tq), :]
        vt = vt_sc[j]
        ss = [lax.dot_general(kb, qm, (((1,), (1,)), ((), ())), preferred_element_type=F32)
              for qm in q_maps]
        return update(ss, [vt] * len(q_maps), carry)

    w2 = 2 * ATT_GRP
    init = ((jnp.full((1, w2), NEG_BIG, F32),) * groups,
            (jnp.zeros((ATT_V_DIM + ATT_ONES, w2), F32),) * groups)
    carry = lax.fori_loop(0, i, full_block, init)

    start = pl.multiple_of(i * tq, tq)
    vt = vt_sc[i]
    ss, vts = [], []
    for g, qm in enumerate(q_maps):
        keys = (g + 1) * ATT_GRP
        kb = kb_sc[pl.ds(start, keys), :]
        s = lax.dot_general(kb, qm, (((1,), (1,)), ((), ())), preferred_element_type=F32)
        key = lax.broadcasted_iota(jnp.int32, (keys, w2), 0)
        qry = lax.broadcasted_iota(jnp.int32, (keys, w2), 1) % ATT_GRP + (keys - ATT_GRP)
        ss.append(jnp.where(key <= qry, s, -jnp.inf))
        vts.append(vt[:, :keys])
    _, accs = update(ss, vts, carry)
    for g in range(groups):
        z = accs[g][:ATT_V_DIM] / accs[g][ATT_V_DIM:ATT_V_DIM + 1]
        o = (z[:, :ATT_GRP] - lam * z[:, ATT_GRP:]).T
        o_ref[g * ATT_GRP:(g + 1) * ATT_GRP, :] = _rms(o, sw_ref[...], SUBLN_EPS) * (1.0 - lam_init)


def _attn_prompt(q, k, v, lam_params, subln_w, batch, seq, lam_init, tq=512):
    tq = min(tq, seq)
    nq = seq // tq
    lam_specs = [_full((1, ATT_QK_DIM))] * 4
    qspec = pl.BlockSpec((tq, LANES), lambda b, h, i: (b * nq + i, h))
    kvspec = pl.BlockSpec((seq, LANES), lambda b, h, i: (b, h))
    return pl.pallas_call(
        functools.partial(_attn_prompt_kernel, tq=tq, lam_init=lam_init),
        grid=(batch, ATT_HEADS, nq),
        in_specs=lam_specs + [_full((1, ATT_V_DIM)), qspec, kvspec, kvspec],
        out_specs=qspec,
        out_shape=jax.ShapeDtypeStruct((batch * seq, ATT_W), F32),
        scratch_shapes=[pltpu.VMEM((seq, LANES), BF16),
                        pltpu.VMEM((nq, ATT_V_DIM + ATT_ONES, tq), BF16)],
        compiler_params=_cparams(("parallel", "parallel", "arbitrary")),
        name="attn_prompt",
    )(*lam_params, subln_w, q, k, v)


def _attn_paged_kernel(pt_ref, lq1, lk1, lq2, lk2, sw_ref, q_ref, kn_ref, vn_ref, *rest,
                       pages_per_step, n_tok, lam_init):
    del pt_ref
    k_refs = rest[:pages_per_step]
    v_refs = rest[pages_per_step:2 * pages_per_step]
    o_ref, m_sc, l_sc, acc_sc = rest[2 * pages_per_step:]
    c = pl.program_id(1)
    half = n_tok * ATT_HEADS
    rows = 2 * half
    page_rows = k_refs[0].shape[1]

    @pl.when(c == 0)
    def _():
        m_sc[...] = jnp.full(m_sc.shape, NEG_BIG, F32)
        l_sc[...] = jnp.zeros(l_sc.shape, F32)
        acc_sc[...] = jnp.zeros(acc_sc.shape, F32)

    row = lax.broadcasted_iota(jnp.int32, (rows, LANES), 0)
    lane = lax.broadcasted_iota(jnp.int32, (rows, LANES), 1)
    qs = jnp.where((lane // ATT_QK_DIM) == (row // half), q_ref[0], 0.0) * (ATT_QK_DIM ** -0.5)
    qs_b = qs.astype(BF16)
    srow = lax.broadcasted_iota(jnp.int32, (rows, page_rows), 0)
    scol = lax.broadcasted_iota(jnp.int32, (rows, page_rows), 1)
    same_head = (scol % ATT_HEADS) == (srow % ATT_HEADS)

    ss = [jnp.where(same_head, _mm_nt(qs_b, kr[0]), -jnp.inf) for kr in k_refs]
    m_old, l_old, acc_old = m_sc[...], l_sc[...], acc_sc[...]
    m = jnp.maximum(m_old, jnp.max(functools.reduce(jnp.maximum, ss), axis=-1, keepdims=True))
    alpha = jnp.exp(m_old - m)
    ps = [jnp.exp(s - m) for s in ss]
    l = alpha * l_old + jnp.sum(functools.reduce(jnp.add, ps), axis=-1, keepdims=True)
    pv = functools.reduce(jnp.add, [_mm(p, vr[0]) for p, vr in zip(ps, v_refs)])
    acc = alpha * acc_old + pv
    m_sc[...], l_sc[...], acc_sc[...] = m, l, acc

    @pl.when(c == pl.num_programs(1) - 1)
    def _():
        lam = _diff_lambda(lq1, lk1, lq2, lk2, lam_init)
        mm, ll, aa = m, l, acc
        r1 = lax.broadcasted_iota(jnp.int32, (rows, 1), 0)
        tok = (r1 // ATT_HEADS) % n_tok
        for j in range(n_tok):
            s = jnp.sum(qs * kn_ref[0, j], axis=-1, keepdims=True)
            s = jnp.where(tok >= j, s, -jnp.inf)
            m_new = jnp.maximum(mm, s)
            alpha = jnp.exp(mm - m_new)
            p = jnp.exp(s - m_new)
            ll = alpha * ll + p
            aa = alpha * aa + p * vn_ref[0, j]
            mm = m_new
        z = aa / ll
        o = z[:half] - lam * z[half:]
        o_ref[0] = _rms(o, sw_ref[...], SUBLN_EPS) * (1.0 - lam_init)


def _attn_paged(q, kn, vn, cache_k, cache_v, page_table, lam_params, subln_w, lam_init,
                pages_per_step=32):
    db, n_tok, _ = q.shape
    n_pages = page_table.shape[1]
    page_rows = cache_k.shape[1]
    pages_per_step = math.gcd(pages_per_step, n_pages)
    steps = n_pages // pages_per_step
    half = n_tok * ATT_HEADS
    rows = 2 * half
    pt_flat = page_table.reshape(-1)
    heads = lambda a: a.reshape(db, n_tok, ATT_HEADS, LANES)
    q_rows = jnp.broadcast_to(heads(q)[:, None], (db, 2, n_tok, ATT_HEADS, LANES)).reshape(db, rows, LANES)
    rep = lambda a: jnp.broadcast_to(
        heads(a)[:, :, None, None], (db, n_tok, 2, n_tok, ATT_HEADS, LANES)).reshape(db, n_tok, rows, LANES)

    def page_spec(i):
        return pl.BlockSpec(
            (1, page_rows, LANES),
            lambda b, c, pt: (pt[b * n_pages + c * pages_per_step + i], 0, 0))

    const = lambda shape: pl.BlockSpec(shape, lambda b, c, pt: (0,) * len(shape))
    new_spec = pl.BlockSpec((1, n_tok, rows, LANES), lambda b, c, pt: (b, 0, 0, 0))
    grid_spec = pltpu.PrefetchScalarGridSpec(
        num_scalar_prefetch=1,
        grid=(db, steps),
        in_specs=[const((1, ATT_QK_DIM))] * 4 + [const((1, ATT_V_DIM))]
        + [pl.BlockSpec((1, rows, LANES), lambda b, c, pt: (b, 0, 0)), new_spec, new_spec]
        + [page_spec(i) for i in range(pages_per_step)] * 2,
        out_specs=pl.BlockSpec((1, half, LANES), lambda b, c, pt: (b, 0, 0)),
        scratch_shapes=[pltpu.VMEM((rows, 1), F32), pltpu.VMEM((rows, 1), F32),
                        pltpu.VMEM((rows, LANES), F32)],
    )
    out = pl.pallas_call(
        functools.partial(_attn_paged_kernel, pages_per_step=pages_per_step, n_tok=n_tok,
                          lam_init=lam_init),
        grid_spec=grid_spec,
        out_shape=jax.ShapeDtypeStruct((db, half, LANES), F32),
        compiler_params=_cparams(("parallel", "arbitrary")),
        name="attn_paged",
    )(pt_flat, *lam_params, subln_w, q_rows, rep(kn), rep(vn),
      *([cache_k] * pages_per_step), *([cache_v] * pages_per_step))
    return out.reshape(db, n_tok, ATT_W)


def _stack(x):
    lane = lax.broadcasted_iota(jnp.int32, x.shape, 1)
    return jnp.concatenate([jnp.where(lane < RW_HEAD, x, 0.0),
                            jnp.where(lane >= RW_HEAD, x, 0.0)], axis=0)


def _each(fn, *lists):
    return [fn(*xs) for xs in zip(*lists)]


def _unit_lower_inverses(lmats, c):
    n = 2 * c
    ri = lax.broadcasted_iota(jnp.int32, (n, n), 0)
    ci = lax.broadcasted_iota(jnp.int32, (n, n), 1)
    eye = jnp.where(ri == ci, 1.0, 0.0)
    same = lambda s: (ri // s) == (ci // s)
    m1 = _each(lambda l: jnp.where(same(INV_BASE), -l, 0.0), lmats)
    m2 = _each(_mm, m1, m1)
    m4 = _each(_mm, m2, m2)
    t = _each(lambda a, b: eye + a + b + _mm(a, b), m1, m2)
    t = _each(lambda a, b: a + _mm(a, b), t, m4)
    s = INV_BASE
    while s < c:
        keep = same(2 * s) & jnp.logical_not(same(s))
        x = _each(lambda l, a: _mm(jnp.where(keep, l, 0.0), a), lmats, t)
        t = _each(lambda a, b: a - _mm(a, b), t, x)
        s *= 2
    return t


def _wkv_scan_kernel(r_ref, lw_ref, k_ref, v_ref, kap_ref, b_ref, s0_ref, tri_ref,
                     y_ref, sout_ref, st_sc, *, c, nseq):
    ci = pl.program_id(1)
    n = 2 * c
    problems = [(g, p) for g in range(nseq) for p in range(RW_PAIRS)]

    @pl.when(ci == 0)
    def _():
        zero = jnp.zeros((RW_HEAD, RW_HEAD), F32)
        for i, (g, p) in enumerate(problems):
            top = jnp.concatenate([s0_ref[g, 2 * p], zero], axis=1)
            bot = jnp.concatenate([zero, s0_ref[g, 2 * p + 1]], axis=1)
            st_sc[i] = jnp.concatenate([top, bot], axis=0)

    ri = lax.broadcasted_iota(jnp.int32, (n, n), 0)
    cj = lax.broadcasted_iota(jnp.int32, (n, n), 1)
    same_head = (ri // c) == (cj // c)
    strict = same_head & (ri > cj)
    incl = same_head & (ri >= cj)
    tri = tri_ref[...]

    sls = [slice(p * LANES, (p + 1) * LANES) for p in range(RW_PAIRS)]
    load = lambda ref: [ref[g, 0, :, sls[p]] for g, p in problems]
    lw, kap, r, k, b, v = (load(ref) for ref in (lw_ref, kap_ref, r_ref, k_ref, b_ref, v_ref))
    dup = lambda x: jnp.concatenate([x, x], axis=0)
    cs = _each(lambda x: _mm_exact_lhs(tri, x), lw)
    cs_end = [x[c - 1:c, :] for x in cs]
    kh = _each(lambda a, x, w: a * jnp.exp(x - w), kap, cs, lw)
    rt = _each(lambda a, x: a * jnp.exp(x), r, cs)
    e_out = [jnp.exp(-x) for x in cs]
    kt = _each(jnp.multiply, k, e_out)
    bt = _each(jnp.multiply, b, e_out)
    e_end = _each(lambda xe, x: jnp.exp(xe - x), cs_end, cs)
    k_end = _each(jnp.multiply, k, e_end)
    b_end = _each(jnp.multiply, b, e_end)

    khs, rts = _each(_stack, kh), _each(_stack, rt)
    bb, kk = _each(dup, bt), _each(dup, kt)
    cat = lambda x, y: jnp.concatenate([x, y], axis=0)
    if n % LANES == 0:
        prod = _each(lambda x1, x2, y1, y2: _mm_nt(cat(x1, x2), cat(y1, y2)), khs, rts, bb, kk)
        quad = lambda i, j: [x[i * n:(i + 1) * n, j * n:(j + 1) * n] for x in prod]
        p_kb, p_kk, p_rb, p_rk = quad(0, 0), quad(0, 1), quad(1, 0), quad(1, 1)
    else:
        p_kb, p_kk = _each(_mm_nt, khs, bb), _each(_mm_nt, khs, kk)
        p_rb, p_rk = _each(_mm_nt, rts, bb), _each(_mm_nt, rts, kk)
    lmat = [jnp.where(strict, x, 0.0) for x in p_kb]
    a_kk = [jnp.where(strict, x, 0.0) for x in p_kk]
    a_rb = [jnp.where(incl, x, 0.0) for x in p_rb]
    a_rk = [jnp.where(incl, x, 0.0) for x in p_rk]
    st = [st_sc[i] for i in range(len(problems))]
    vs = _each(_stack, v)
    from_state = _each(lambda x, y, s: _mm_nt(cat(x, y), s), kh, rt, st)
    from_v = _each(lambda a1, a2, y: _mm(cat(a1, a2), y), a_kk, a_rk, vs)
    rhs = _each(lambda s, a: _stack(s[:c]) + a[:n], from_state, from_v)
    y0 = _each(lambda s, a: _stack(s[c:]) + a[n:], from_state, from_v)
    tinv = _unit_lower_inverses(lmat, c)
    u = _each(_mm, tinv, rhs)
    ys = _each(lambda y, a, x: y - _mm(a, x), y0, a_rb, u)
    vu = _each(lambda x, y: jnp.concatenate([x, y], axis=0), vs, u)
    kb = _each(lambda x, y: jnp.concatenate([_stack(x), -_stack(y)], axis=0), k_end, b_end)
    st_new = _each(lambda s, xe, x, y: s * jnp.exp(xe) + _mm(x.T, y), st, cs_end, vu, kb)
    for i, (g, p) in enumerate(problems):
        y_ref[g, 0, :, sls[p]] = ys[i][:c] + ys[i][c:]
        st_sc[i] = st_new[i]

    @pl.when(ci == pl.num_programs(1) - 1)
    def _():
        for i, (g, p) in enumerate(problems):
            st = st_sc[i]
            sout_ref[g, 2 * p] = st[:RW_HEAD, :RW_HEAD]
            sout_ref[g, 2 * p + 1] = st[RW_HEAD:, RW_HEAD:]


def _wkv_scan(r, lw, k, v, kap, b, s0, batch, seq, c, nseq=WKV_SEQS):
    nc = seq // c
    nseq = math.gcd(nseq, batch)
    tile = pl.BlockSpec((nseq, 1, c, RW_W), lambda bi, ci: (bi, ci, 0, 0))
    sspec = pl.BlockSpec((nseq, RW_HEADS, RW_HEAD, RW_HEAD), lambda bi, ci: (bi, 0, 0, 0))
    tri = jnp.tril(jnp.ones((c, c), F32)).astype(BF16)
    tiles = [t.reshape(batch, nc, c, RW_W) for t in (r, lw, k, v, kap, b)]
    y, s_new = pl.pallas_call(
        functools.partial(_wkv_scan_kernel, c=c, nseq=nseq),
        grid=(batch // nseq, nc),
        in_specs=[tile] * 6 + [sspec, _full((c, c))],
        out_specs=[tile, sspec],
        out_shape=[jax.ShapeDtypeStruct((batch, nc, c, RW_W), F32),
                   jax.ShapeDtypeStruct((batch, RW_HEADS, RW_HEAD, RW_HEAD), F32)],
        scratch_shapes=[pltpu.VMEM((nseq * RW_PAIRS, LANES, LANES), F32)],
        compiler_params=_cparams(("parallel", "arbitrary")),
        name="wkv_scan",
    )(*tiles, s0, tri)
    return y.reshape(batch * seq, RW_W), s_new


def _mix_out_kernel(x_ref, att_ref, y_ref, bonus_ref, g_ref, lnw_ref, lnb_ref, seg_ref,
                    woa_ref, wor_ref, gpost_ref, o_ref):
    y = y_ref[...]
    seg = seg_ref[...]
    mean = _seg_sum(y, seg) * (1.0 / RW_HEAD)
    d = y - mean
    var = _seg_sum(d * d, seg) * (1.0 / RW_HEAD)
    yn = d * lax.rsqrt(var + GN_EPS) * lnw_ref[...] + lnb_ref[...]
    rw_out = (yn + bonus_ref[...]) * g_ref[...]
    mix = _mm(att_ref[...], woa_ref[...]) + _mm(rw_out, wor_ref[...])
    o_ref[...] = x_ref[...] + _rms(mix, gpost_ref[...], RMS_EPS)


def _mix_out(x, att, y, bonus, g, p, tm):
    m = x.shape[0]
    row = lambda w: pl.BlockSpec((tm, w), lambda i: (i, 0))
    vec = _full((1, RW_W))
    return pl.pallas_call(
        _mix_out_kernel,
        grid=(m // tm,),
        in_specs=[row(D_MODEL), row(ATT_W), row(RW_W), row(RW_W), row(RW_W), vec, vec,
                  _full((RW_W, RW_W)), _full((ATT_W, D_MODEL)), _full((RW_W, D_MODEL)),
                  _full((1, D_MODEL))],
        out_specs=row(D_MODEL),
        out_shape=jax.ShapeDtypeStruct((m, D_MODEL), F32),
        compiler_params=_cparams(("parallel",)),
        name="mix_out",
    )(x, att, y, bonus, g, p['lnx_w'], p['lnx_b'], p['seg'], p['w_out_att'], p['w_out_rw'],
      p['norm_mix_post'])


def _ffn_kernel(x_ref, gpre_ref, gpost_ref, wg_ref, wu_ref, wd_ref, o_ref):
    x = x_ref[...]
    f = _rms(x, gpre_ref[...], RMS_EPS).astype(BF16)
    dot = functools.partial(jnp.dot, preferred_element_type=F32)
    tff = D_FF // FFN_CHUNKS
    acc = None
    for j in range(FFN_CHUNKS):
        cs = slice(j * tff, (j + 1) * tff)
        gate = dot(f, wg_ref[:, cs])
        up = dot(f, wu_ref[:, cs])
        h = gate * _sigmoid(gate) * up
        part = dot(h.astype(BF16), wd_ref[cs, :])
        acc = part if acc is None else acc + part
    o_ref[...] = x + _rms(acc, gpost_ref[...], RMS_EPS)


def _ffn(x, p, tm):
    m = x.shape[0]
    row = pl.BlockSpec((tm, D_MODEL), lambda i: (i, 0))
    return pl.pallas_call(
        _ffn_kernel,
        grid=(m // tm,),
        in_specs=[row, _const((1, D_MODEL)), _const((1, D_MODEL)),
                  _const((D_MODEL, D_FF)), _const((D_MODEL, D_FF)), _const((D_FF, D_MODEL))],
        out_specs=row,
        out_shape=jax.ShapeDtypeStruct((m, D_MODEL), F32),
        compiler_params=_cparams(("parallel",)),
        name="ffn",
    )(x, p['norm_ffn_pre'], p['norm_ffn_post'], p['ffn_gate'], p['ffn_up'], p['ffn_down'])


def _layer_params(l, w):
    row = lambda a: a[l].reshape(1, -1).astype(F32)
    zeros64 = jnp.zeros((64, RW_W), F32)
    head = jnp.arange(RW_W) // RW_HEAD
    return {
        'norm_mix_pre': row(w['norm_mix_pre']), 'norm_mix_post': row(w['norm_mix_post']),
        'norm_ffn_pre': row(w['norm_ffn_pre']), 'norm_ffn_post': row(w['norm_ffn_post']),
        'w_in': w['w_in'][l].astype(BF16),
        'w_out_att': w['w_out'][l][:ATT_W].astype(BF16),
        'w_out_rw': w['w_out'][l][ATT_W:].astype(BF16),
        'lam': [row(w[n]) for n in ('lambda_q1', 'lambda_k1', 'lambda_q2', 'lambda_k2')],
        'subln_w': row(w['subln_w']),
        'mu': row(w['rw_mu']), 'w0': row(w['rw_w0']), 'a0': row(w['rw_a0']),
        'kk': row(w['rw_kk']), 'ka': row(w['rw_ka']), 'rk': row(w['rw_rk']),
        'lnx_w': row(w['rw_lnx_w']), 'lnx_b': row(w['rw_lnx_b']),
        'w2pad': jnp.concatenate([w['rw_w2'][l], zeros64], axis=0).astype(BF16),
        'a2pad': jnp.concatenate([zeros64, w['rw_a2'][l]], axis=0).astype(BF16),
        'g2': w['rw_g2'][l].astype(BF16),
        'seg': (head[:, None] == head[None, :]).astype(BF16),
        'ffn_gate': w['ffn_gate'][l].astype(BF16), 'ffn_up': w['ffn_up'][l].astype(BF16),
        'ffn_down': w['ffn_down'][l].astype(BF16),
    }


def _layer(x, shift0, wkv0, attend, p, batch, seq):
    m = batch * seq
    if seq >= WKV_CHUNK:
        tm = min(ROW_TILE, seq)
        tiles = seq // tm
        first = jnp.zeros((batch, tiles, RW_PROJ_W), F32).at[:, 0, :].set(shift0).reshape(m // tm, 1, RW_PROJ_W)
        period, c = tm, WKV_CHUNK
    else:
        tm = min(ROW_TILE, m)
        tiles = 1
        first = jnp.zeros((batch, seq, RW_PROJ_W), F32).at[:, 0, :].set(shift0).reshape(m // tm, tm, RW_PROJ_W)
        period, c = seq, 8
    q, k, v, k4, v4, r, lw, k2, vr, kap, b, gate, bonus, tail = _inproj(x, first, period, tiles, p, tm)
    att = attend(q, k, v)
    scan_in = (r, lw, k2, vr, kap, b)
    if c > seq:
        pad = lambda t: jnp.pad(t.reshape(batch, seq, RW_W), ((0, 0), (0, c - seq), (0, 0))).reshape(batch * c, RW_W)
        y, wkv_new = _wkv_scan(*[pad(t) for t in scan_in], wkv0, batch, c, c)
        y = y.reshape(batch, c, RW_W)[:, :seq].reshape(m, RW_W)
        shift_new = tail.reshape(batch, seq, RW_PROJ_W)[:, -1]
    else:
        y, wkv_new = _wkv_scan(*scan_in, wkv0, batch, seq, c)
        shift_new = tail.reshape(batch, tiles, RW_PROJ_W)[:, -1]
    x1 = _mix_out(x, att, y, bonus, gate, p, tm)
    out = _ffn(x1, p, tm)
    return out, k4, v4, wkv_new, shift_new


def kernel(x_prompt, x_sample, cache_k, cache_v, state_wkv, state_shift, page_table, norm_mix_pre, norm_mix_post, norm_ffn_pre, norm_ffn_post, w_in, w_out, lambda_q1, lambda_k1, lambda_q2, lambda_k2, subln_w, rw_mu, rw_w0, rw_w2, rw_a0, rw_a2, rw_g2, rw_kk, rw_ka, rw_rk, rw_lnx_w, rw_lnx_b, ffn_gate, ffn_up, ffn_down):
    weights = dict(
        norm_mix_pre=norm_mix_pre, norm_mix_post=norm_mix_post, norm_ffn_pre=norm_ffn_pre,
        norm_ffn_post=norm_ffn_post, w_in=w_in, w_out=w_out, lambda_q1=lambda_q1,
        lambda_k1=lambda_k1, lambda_q2=lambda_q2, lambda_k2=lambda_k2, subln_w=subln_w,
        rw_mu=rw_mu, rw_w0=rw_w0, rw_w2=rw_w2, rw_a0=rw_a0, rw_a2=rw_a2, rw_g2=rw_g2,
        rw_kk=rw_kk, rw_ka=rw_ka, rw_rk=rw_rk, rw_lnx_w=rw_lnx_w, rw_lnx_b=rw_lnx_b,
        ffn_gate=ffn_gate, ffn_up=ffn_up, ffn_down=ffn_down)
    depth = w_in.shape[0]
    batch, seq, _ = x_prompt.shape
    db, dseq, _ = x_sample.shape
    n_pool, page = cache_k.shape[1], cache_k.shape[2]
    yp = x_prompt.reshape(batch * seq, D_MODEL)
    ys = x_sample.reshape(db * dseq, D_MODEL)
    outs = [[] for _ in range(8)]
    for l in range(depth):
        lam_init = 0.8 - 0.6 * math.exp(-0.3 * l)
        p = _layer_params(l, weights)
        ck = cache_k[l].reshape(n_pool, page * ATT_HEADS, LANES)
        cv = cache_v[l].reshape(n_pool, page * ATT_HEADS, LANES)

        def attend_p(q, k, v):
            return _attn_prompt(q, k, v, p['lam'], p['subln_w'], batch, seq, lam_init)

        def attend_s(q, k, v):
            t3 = lambda a: a.reshape(db, dseq, ATT_W)
            o = _attn_paged(t3(q), t3(k), t3(v), ck, cv, page_table, p['lam'], p['subln_w'], lam_init)
            return o.reshape(db * dseq, ATT_W)

        zshift = jnp.zeros((batch, RW_PROJ_W), F32)
        zwkv = jnp.zeros((batch, RW_HEADS, RW_HEAD, RW_HEAD), F32)
        yp, kp, vp, wp, sp = _layer(yp, zshift, zwkv, attend_p, p, batch, seq)
        ys, ks, vs, ws, ss = _layer(ys, state_shift[l], state_wkv[l], attend_s, p, db, dseq)
        kv_p = lambda a: a.reshape(batch, seq, ATT_HEADS, ATT_V_DIM)
        kv_s = lambda a: a.reshape(db, dseq, ATT_HEADS, ATT_V_DIM)
        for lst, val in zip(outs, (kv_p(kp), kv_p(vp), wp, sp, kv_s(ks), kv_s(vs), ws, ss)):
            lst.append(val)
    stacked = [jnp.stack(o) for o in outs]
    return (yp.reshape(batch, seq, D_MODEL), ys.reshape(db, dseq, D_MODEL), *stacked)
```

```python
import functools
import math

import jax
import jax.numpy as jnp
from jax import lax
from jax.experimental import pallas as pl
from jax.experimental.pallas import tpu as pltpu

F32 = jnp.float32
BF16 = jnp.bfloat16

D_MODEL = 1024
ATT_W = 512
RW_W = 512
ATT_HEADS = 4
ATT_QK_DIM = 64
ATT_V_DIM = 128
RW_HEAD = 64
RW_HEADS = 8
RW_PAIRS = RW_HEADS // 2
LORA_OFF = 3 * RW_W
GATE_OFF = LORA_OFF + 128
RW_PROJ_W = 1792
IN_W = 3 * ATT_W + RW_PROJ_W
D_FF = 2816
RMS_EPS = 1e-6
SUBLN_EPS = 1e-5
GN_EPS = 64e-5
NEG_BIG = -1e30

LANES = 128
VMEM_LIMIT = 56 * 1024 * 1024

ROW_TILE = 512
FFN_CHUNKS = 2
WKV_CHUNK = 64
WKV_SEQS = 4
INV_BASE = 8


def _cparams(sem):
    return pltpu.CompilerParams(dimension_semantics=sem, vmem_limit_bytes=VMEM_LIMIT)


def _full(shape):
    return pl.BlockSpec(shape, lambda *_: (0,) * len(shape))


def _mm(a, b):
    return jnp.dot(a.astype(BF16), b.astype(BF16), preferred_element_type=F32)


def _mm_nt(a, b):
    return lax.dot_general(a.astype(BF16), b.astype(BF16), (((1,), (1,)), ((), ())),
                           preferred_element_type=F32)


def _split2(x):
    hi = x.astype(BF16)
    lo = (x - hi.astype(F32)).astype(BF16)
    return hi, lo


def _split3(x):
    hi = x.astype(BF16)
    r1 = x - hi.astype(F32)
    mid = r1.astype(BF16)
    lo = (r1 - mid.astype(F32)).astype(BF16)
    return hi, mid, lo


def _seg_sum(a, seg_bf16):
    hi, lo = _split2(a)
    d = functools.partial(jnp.dot, preferred_element_type=F32)
    return d(hi, seg_bf16) + d(lo, seg_bf16)


def _const(shape):
    return pl.BlockSpec(shape, lambda *_: (0,) * len(shape), pipeline_mode=pl.Buffered(1))


def _mm_exact_lhs(a_bf16, b):
    h1, h2, h3 = _split3(b)
    d = functools.partial(jnp.dot, preferred_element_type=F32)
    return d(a_bf16, h1) + d(a_bf16, h2) + d(a_bf16, h3)


def _rms(x, g, eps):
    return x * lax.rsqrt(jnp.mean(x * x, axis=-1, keepdims=True) + eps) * g


def _diff_lambda(lq1, lk1, lq2, lk2, lam_init):
    s1 = jnp.sum(lq1[...] * lk1[...], axis=-1, keepdims=True)
    s2 = jnp.sum(lq2[...] * lk2[...], axis=-1, keepdims=True)
    return jnp.exp(s1) - jnp.exp(s2) + lam_init


def _softplus(x):
    return jnp.maximum(x, 0.0) + jnp.log1p(jnp.exp(-jnp.abs(x)))


def _sigmoid(x):
    return 1.0 / (1.0 + jnp.exp(-x))


def _inproj_kernel(x_ref, first_ref, g_ref, w_ref, mu_ref, w0_ref, a0_ref, kkw_ref, ka_ref, rk_ref,
                   w2_ref, a2_ref, g2_ref, seg_ref,
                   q_ref, k_ref, v_ref, k4_ref, v4_ref,
                   r_ref, lw_ref, k2_ref, vr_ref, kap_ref, b_ref, gate_ref, bonus_ref, tail_ref,
                   last_sc, *, period, tiles_per_seq):
    i = pl.program_id(0)
    tm = x_ref.shape[0]
    dot = functools.partial(jnp.dot, preferred_element_type=F32)
    carried = first_ref.shape[1] == 1
    halves = [slice(0, tm // 2), slice(tm // 2, tm)]
    xh = []
    for rs in halves:
        n = rs.stop - rs.start
        h = _rms(x_ref[rs, :], g_ref[...], RMS_EPS).astype(BF16)
        q_ref[rs, :] = dot(h, w_ref[:, 0:ATT_W])
        k = dot(h, w_ref[:, ATT_W:2 * ATT_W])
        v = dot(h, w_ref[:, 2 * ATT_W:3 * ATT_W])
        k_ref[rs, :] = k
        v_ref[rs, :] = v
        for hd in range(ATT_HEADS):
            sl = slice(hd * LANES, (hd + 1) * LANES)
            k4_ref[pl.ds(rs.start * ATT_HEADS + hd, n, stride=ATT_HEADS), :] = k[:, sl]
            v4_ref[pl.ds(rs.start * ATT_HEADS + hd, n, stride=ATT_HEADS), :] = v[:, sl]
        xh.append(dot(h, w_ref[:, 3 * ATT_W:]))

    if carried:
        @pl.when(i == 0)
        def _():
            last_sc[...] = jnp.zeros(last_sc.shape, F32)
        lead = jnp.where(i % tiles_per_seq == 0, first_ref[0], last_sc[...])
        last_sc[...] = xh[-1][-1:, :]
        tail_ref[0] = xh[-1][-1:, :]

    for idx, rs in enumerate(halves):
        x = xh[idx]
        row = lax.broadcasted_iota(jnp.int32, x.shape, 0)
        prev = pltpu.roll(x, 1, axis=0)
        if carried:
            prev = jnp.where(row == 0, lead if idx == 0 else xh[idx - 1][-1:, :], prev)
        else:
            prev = jnp.where(row % period == 0, first_ref[0, rs, :], prev)
            tail_ref[rs, :] = x
        xs = x + (prev - x) * mu_ref[...]
        r = xs[:, 0:RW_W]
        kx = xs[:, RW_W:2 * RW_W]
        vx = xs[:, 2 * RW_W:3 * RW_W]
        lo = xs[:, LORA_OFF:LORA_OFF + 128]
        g_lo = xs[:, GATE_OFF:GATE_OFF + 128]
        w_raw = -_softplus(-(w0_ref[...] + _mm(jnp.tanh(lo), w2_ref[...]))) - 0.5
        lw_ref[rs, :] = -jnp.exp(w_raw)
        a = _sigmoid(a0_ref[...] + _mm(lo, a2_ref[...]))
        gate_ref[rs, :] = _mm(_sigmoid(g_lo), g2_ref[...])
        kk = kx * kkw_ref[...]
        norm = jnp.sqrt(_seg_sum(kk * kk, seg_ref[...]))
        kap = kk / jnp.maximum(norm, 1e-12)
        k2 = kx * (1.0 + (a - 1.0) * ka_ref[...])
        r_ref[rs, :] = r
        k2_ref[rs, :] = k2
        vr_ref[rs, :] = vx
        kap_ref[rs, :] = kap
        b_ref[rs, :] = kap * a
        bonus_ref[rs, :] = _seg_sum(r * k2 * rk_ref[...], seg_ref[...]) * vx


def _inproj(x, first, period, tiles_per_seq, p, tm):
    m = x.shape[0]
    row = lambda w: pl.BlockSpec((tm, w), lambda i: (i, 0))
    heads_row = pl.BlockSpec((tm * ATT_HEADS, LANES), lambda i: (i, 0))
    vec = _const((1, RW_W))
    first_spec = pl.BlockSpec((1,) + first.shape[1:], lambda i: (i, 0, 0))
    wide = jax.ShapeDtypeStruct((m, RW_W), F32)
    if first.shape[1] == 1:
        tail_spec = pl.BlockSpec((1, 1, RW_PROJ_W), lambda i: (i, 0, 0))
        tail_shape = jax.ShapeDtypeStruct((m // tm, 1, RW_PROJ_W), F32)
    else:
        tail_spec = row(RW_PROJ_W)
        tail_shape = jax.ShapeDtypeStruct((m, RW_PROJ_W), F32)
    return pl.pallas_call(
        functools.partial(_inproj_kernel, period=period, tiles_per_seq=tiles_per_seq),
        grid=(m // tm,),
        in_specs=[row(D_MODEL), first_spec, _const((1, D_MODEL)), _const((D_MODEL, IN_W)),
                  _const((1, RW_PROJ_W)), vec, vec, vec, vec, vec,
                  _const((128, RW_W)), _const((128, RW_W)), _const((128, RW_W)), _const((RW_W, RW_W))],
        out_specs=[row(ATT_W)] * 3 + [heads_row] * 2 + [row(RW_W)] * 8 + [tail_spec],
        out_shape=[wide] * 3 + [jax.ShapeDtypeStruct((m * ATT_HEADS, LANES), F32)] * 2 + [wide] * 8
        + [tail_shape],
        scratch_shapes=[pltpu.VMEM((1, RW_PROJ_W), F32)],
        compiler_params=_cparams(("arbitrary",)),
        name="inproj",
    )(x, first, p['norm_mix_pre'], p['w_in'], p['mu'], p['w0'], p['a0'], p['kk'], p['ka'], p['rk'],
      p['w2pad'], p['a2pad'], p['g2'], p['seg'])


ATT_GRP = 128
ATT_ONES = 16
LOG2E = 1.4426950408889634


def _attn_prompt_kernel(lq1, lk1, lq2, lk2, sw_ref, q_ref, k_ref, v_ref, o_ref, kb_sc, vt_sc,
                        *, tq, lam_init):
    i = pl.program_id(2)
    groups = tq // ATT_GRP
    n_blk = k_ref.shape[0] // tq

    @pl.when(i == 0)
    def _():
        kb_sc[...] = k_ref[...].astype(BF16)
        ones = jnp.ones((ATT_ONES, tq), BF16)
        for jb in range(n_blk):
            vt_sc[jb] = jnp.concatenate([v_ref[jb * tq:(jb + 1) * tq, :].T.astype(BF16), ones], axis=0)

    lam = _diff_lambda(lq1, lk1, lq2, lk2, lam_init)
    lane = lax.broadcasted_iota(jnp.int32, (ATT_GRP, LANES), 1)
    q_maps = []
    for g in range(groups):
        q = q_ref[g * ATT_GRP:(g + 1) * ATT_GRP, :] * (ATT_QK_DIM ** -0.5 * LOG2E)
        q_maps.append(jnp.concatenate([jnp.where(lane < ATT_QK_DIM, q, 0.0),
                                       jnp.where(lane >= ATT_QK_DIM, q, 0.0)], axis=0).astype(BF16))

    def update(ss, vts, carry):
        ms, accs = carry
        m_new = _each(lambda m, s: jnp.maximum(m, jnp.max(s, axis=0, keepdims=True)), ms, ss)
        alpha = _each(lambda m, mn: jnp.exp2(m - mn), ms, m_new)
        ps = _each(lambda s, mn: jnp.exp2(s - mn).astype(BF16), ss, m_new)
        pv = _each(lambda vt, p: jnp.dot(vt, p, preferred_element_type=F32), vts, ps)
        accs = _each(lambda a, acc, x: a * acc + x, alpha, accs, pv)
        return tuple(m_new), tuple(accs)

    def full_block(j, carry):
        kb = kb_sc[pl.ds(pl.multiple_of(j * tq, tq), tq), :]
        vt = vt_sc[j]
        ss = [lax.dot_general(kb, qm, (((1,), (1,)), ((), ())), preferred_element_type=F32)
              for qm in q_maps]
        return update(ss, [vt] * len(q_maps), carry)

    w2 = 2 * ATT_GRP
    init = ((jnp.full((1, w2), NEG_BIG, F32),) * groups,
            (jnp.zeros((ATT_V_DIM + ATT_ONES, w2), F32),) * groups)
    carry = lax.fori_loop(0, i, full_block, init)

    start = pl.multiple_of(i * tq, tq)
    vt = vt_sc[i]
    ss, vts = [], []
    for g, qm in enumerate(q_maps):
        keys = (g + 1) * ATT_GRP
        kb = kb_sc[pl.ds(start, keys), :]
        s = lax.dot_general(kb, qm, (((1,), (1,)), ((), ())), preferred_element_type=F32)
        key = lax.broadcasted_iota(jnp.int32, (keys, w2), 0)
        qry = lax.broadcasted_iota(jnp.int32, (keys, w2), 1) % ATT_GRP + (keys - ATT_GRP)
        ss.append(jnp.where(key <= qry, s, -jnp.inf))
        vts.append(vt[:, :keys])
    _, accs = update(ss, vts, carry)
    for g in range(groups):
        z = accs[g][:ATT_V_DIM] / accs[g][ATT_V_DIM:ATT_V_DIM + 1]
        o = (z[:, :ATT_GRP] - lam * z[:, ATT_GRP:]).T
        o_ref[g * ATT_GRP:(g + 1) * ATT_GRP, :] = _rms(o, sw_ref[...], SUBLN_EPS) * (1.0 - lam_init)


def _attn_prompt(q, k, v, lam_params, subln_w, batch, seq, lam_init, tq=512):
    tq = min(tq, seq)
    nq = seq // tq
    lam_specs = [_full((1, ATT_QK_DIM))] * 4
    qspec = pl.BlockSpec((tq, LANES), lambda b, h, i: (b * nq + i, h))
    kvspec = pl.BlockSpec((seq, LANES), lambda b, h, i: (b, h))
    return pl.pallas_call(
        functools.partial(_attn_prompt_kernel, tq=tq, lam_init=lam_init),
        grid=(batch, ATT_HEADS, nq),
        in_specs=lam_specs + [_full((1, ATT_V_DIM)), qspec, kvspec, kvspec],
        out_specs=qspec,
        out_shape=jax.ShapeDtypeStruct((batch * seq, ATT_W), F32),
        scratch_shapes=[pltpu.VMEM((seq, LANES), BF16),
                        pltpu.VMEM((nq, ATT_V_DIM + ATT_ONES, tq), BF16)],
        compiler_params=_cparams(("parallel", "parallel", "arbitrary")),
        name="attn_prompt",
    )(*lam_params, subln_w, q, k, v)


def _attn_paged_kernel(pt_ref, lq1, lk1, lq2, lk2, sw_ref, q_ref, kn_ref, vn_ref, *rest,
                       pages_per_step, n_tok, lam_init):
    del pt_ref
    k_refs = rest[:pages_per_step]
    v_refs = rest[pages_per_step:2 * pages_per_step]
    o_ref, m_sc, l_sc, acc_sc = rest[2 * pages_per_step:]
    c = pl.program_id(1)
    half = n_tok * ATT_HEADS
    rows = 2 * half
    page_rows = k_refs[0].shape[1]

    @pl.when(c == 0)
    def _():
        m_sc[...] = jnp.full(m_sc.shape, NEG_BIG, F32)
        l_sc[...] = jnp.zeros(l_sc.shape, F32)
        acc_sc[...] = jnp.zeros(acc_sc.shape, F32)

    row = lax.broadcasted_iota(jnp.int32, (rows, LANES), 0)
    lane = lax.broadcasted_iota(jnp.int32, (rows, LANES), 1)
    qs = jnp.where((lane // ATT_QK_DIM) == (row // half), q_ref[0], 0.0) * (ATT_QK_DIM ** -0.5)
    qs_b = qs.astype(BF16)
    srow = lax.broadcasted_iota(jnp.int32, (rows, page_rows), 0)
    scol = lax.broadcasted_iota(jnp.int32, (rows, page_rows), 1)
    same_head = (scol % ATT_HEADS) == (srow % ATT_HEADS)

    ss = [jnp.where(same_head, _mm_nt(qs_b, kr[0]), -jnp.inf) for kr in k_refs]
    m_old, l_old, acc_old = m_sc[...], l_sc[...], acc_sc[...]
    m = jnp.maximum(m_old, jnp.max(functools.reduce(jnp.maximum, ss), axis=-1, keepdims=True))
    alpha = jnp.exp(m_old - m)
    ps = [jnp.exp(s - m) for s in ss]
    l = alpha * l_old + jnp.sum(functools.reduce(jnp.add, ps), axis=-1, keepdims=True)
    pv = functools.reduce(jnp.add, [_mm(p, vr[0]) for p, vr in zip(ps, v_refs)])
    acc = alpha * acc_old + pv
    m_sc[...], l_sc[...], acc_sc[...] = m, l, acc

    @pl.when(c == pl.num_programs(1) - 1)
    def _():
        lam = _diff_lambda(lq1, lk1, lq2, lk2, lam_init)
        mm, ll, aa = m, l, acc
        r1 = lax.broadcasted_iota(jnp.int32, (rows, 1), 0)
        tok = (r1 // ATT_HEADS) % n_tok
        for j in range(n_tok):
            s = jnp.sum(qs * kn_ref[0, j], axis=-1, keepdims=True)
            s = jnp.where(tok >= j, s, -jnp.inf)
            m_new = jnp.maximum(mm, s)
            alpha = jnp.exp(mm - m_new)
            p = jnp.exp(s - m_new)
            ll = alpha * ll + p
            aa = alpha * aa + p * vn_ref[0, j]
            mm = m_new
        z = aa / ll
        o = z[:half] - lam * z[half:]
        o_ref[0] = _rms(o, sw_ref[...], SUBLN_EPS) * (1.0 - lam_init)


def _attn_paged(q, kn, vn, cache_k, cache_v, page_table, lam_params, subln_w, lam_init,
                pages_per_step=32):
    db, n_tok, _ = q.shape
    n_pages = page_table.shape[1]
    page_rows = cache_k.shape[1]
    pages_per_step = math.gcd(pages_per_step, n_pages)
    steps = n_pages // pages_per_step
    half = n_tok * ATT_HEADS
    rows = 2 * half
    pt_flat = page_table.reshape(-1)
    heads = lambda a: a.reshape(db, n_tok, ATT_HEADS, LANES)
    q_rows = jnp.broadcast_to(heads(q)[:, None], (db, 2, n_tok, ATT_HEADS, LANES)).reshape(db, rows, LANES)
    rep = lambda a: jnp.broadcast_to(
        heads(a)[:, :, None, None], (db, n_tok, 2, n_tok, ATT_HEADS, LANES)).reshape(db, n_tok, rows, LANES)

    def page_spec(i):
        return pl.BlockSpec(
            (1, page_rows, LANES),
            lambda b, c, pt: (pt[b * n_pages + c * pages_per_step + i], 0, 0))

    const = lambda shape: pl.BlockSpec(shape, lambda b, c, pt: (0,) * len(shape))
    new_spec = pl.BlockSpec((1, n_tok, rows, LANES), lambda b, c, pt: (b, 0, 0, 0))
    grid_spec = pltpu.PrefetchScalarGridSpec(
        num_scalar_prefetch=1,
        grid=(db, steps),
        in_specs=[const((1, ATT_QK_DIM))] * 4 + [const((1, ATT_V_DIM))]
        + [pl.BlockSpec((1, rows, LANES), lambda b, c, pt: (b, 0, 0)), new_spec, new_spec]
        + [page_spec(i) for i in range(pages_per_step)] * 2,
        out_specs=pl.BlockSpec((1, half, LANES), lambda b, c, pt: (b, 0, 0)),
        scratch_shapes=[pltpu.VMEM((rows, 1), F32), pltpu.VMEM((rows, 1), F32),
                        pltpu.VMEM((rows, LANES), F32)],
    )
    out = pl.pallas_call(
        functools.partial(_attn_paged_kernel, pages_per_step=pages_per_step, n_tok=n_tok,
                          lam_init=lam_init),
        grid_spec=grid_spec,
        out_shape=jax.ShapeDtypeStruct((db, half, LANES), F32),
        compiler_params=_cparams(("parallel", "arbitrary")),
        name="attn_paged",
    )(pt_flat, *lam_params, subln_w, q_rows, rep(kn), rep(vn),
      *([cache_k] * pages_per_step), *([cache_v] * pages_per_step))
    return out.reshape(db, n_tok, ATT_W)


def _stack(x):
    lane = lax.broadcasted_iota(jnp.int32, x.shape, 1)
    return jnp.concatenate([jnp.where(lane < RW_HEAD, x, 0.0),
                            jnp.where(lane >= RW_HEAD, x, 0.0)], axis=0)


def _each(fn, *lists):
    return [fn(*xs) for xs in zip(*lists)]


def _unit_lower_inverses(lmats, c):
    n = 2 * c
    ri = lax.broadcasted_iota(jnp.int32, (n, n), 0)
    ci = lax.broadcasted_iota(jnp.int32, (n, n), 1)
    eye = jnp.where(ri == ci, 1.0, 0.0)
    same = lambda s: (ri // s) == (ci // s)
    m1 = _each(lambda l: jnp.where(same(INV_BASE), -l, 0.0), lmats)
    m2 = _each(_mm, m1, m1)
    m4 = _each(_mm, m2, m2)
    t = _each(lambda a, b: eye + a + b + _mm(a, b), m1, m2)
    t = _each(lambda a, b: a + _mm(a, b), t, m4)
    s = INV_BASE
    while s < c:
        keep = same(2 * s) & jnp.logical_not(same(s))
        x = _each(lambda l, a: _mm(jnp.where(keep, l, 0.0), a), lmats, t)
        t = _each(lambda a, b: a - _mm(a, b), t, x)
        s *= 2
    return t


def _wkv_scan_kernel(r_ref, lw_ref, k_ref, v_ref, kap_ref, b_ref, s0_ref, tri_ref,
                     y_ref, sout_ref, st_sc, *, c, nseq):
    ci = pl.program_id(1)
    n = 2 * c
    problems = [(g, p) for g in range(nseq) for p in range(RW_PAIRS)]

    @pl.when(ci == 0)
    def _():
        zero = jnp.zeros((RW_HEAD, RW_HEAD), F32)
        for i, (g, p) in enumerate(problems):
            top = jnp.concatenate([s0_ref[g, 2 * p], zero], axis=1)
            bot = jnp.concatenate([zero, s0_ref[g, 2 * p + 1]], axis=1)
            st_sc[i] = jnp.concatenate([top, bot], axis=0)

    ri = lax.broadcasted_iota(jnp.int32, (n, n), 0)
    cj = lax.broadcasted_iota(jnp.int32, (n, n), 1)
    same_head = (ri // c) == (cj // c)
    strict = same_head & (ri > cj)
    incl = same_head & (ri >= cj)
    tri = tri_ref[...]

    sls = [slice(p * LANES, (p + 1) * LANES) for p in range(RW_PAIRS)]
    load = lambda ref: [ref[g, 0, :, sls[p]] for g, p in problems]
    lw, kap, r, k, b, v = (load(ref) for ref in (lw_ref, kap_ref, r_ref, k_ref, b_ref, v_ref))
    dup = lambda x: jnp.concatenate([x, x], axis=0)
    cs = _each(lambda x: _mm_exact_lhs(tri, x), lw)
    cs_end = [x[c - 1:c, :] for x in cs]
    kh = _each(lambda a, x, w: a * jnp.exp(x - w), kap, cs, lw)
    rt = _each(lambda a, x: a * jnp.exp(x), r, cs)
    e_out = [jnp.exp(-x) for x in cs]
    kt = _each(jnp.multiply, k, e_out)
    bt = _each(jnp.multiply, b, e_out)
    e_end = _each(lambda xe, x: jnp.exp(xe - x), cs_end, cs)
    k_end = _each(jnp.multiply, k, e_end)
    b_end = _each(jnp.multiply, b, e_end)

    khs, rts = _each(_stack, kh), _each(_stack, rt)
    bb, kk = _each(dup, bt), _each(dup, kt)
    cat = lambda x, y: jnp.concatenate([x, y], axis=0)
    if n % LANES == 0:
        prod = _each(lambda x1, x2, y1, y2: _mm_nt(cat(x1, x2), cat(y1, y2)), khs, rts, bb, kk)
        quad = lambda i, j: [x[i * n:(i + 1) * n, j * n:(j + 1) * n] for x in prod]
        p_kb, p_kk, p_rb, p_rk = quad(0, 0), quad(0, 1), quad(1, 0), quad(1, 1)
    else:
        p_kb, p_kk = _each(_mm_nt, khs, bb), _each(_mm_nt, khs, kk)
        p_rb, p_rk = _each(_mm_nt, rts, bb), _each(_mm_nt, rts, kk)
    lmat = [jnp.where(strict, x, 0.0) for x in p_kb]
    a_kk = [jnp.where(strict, x, 0.0) for x in p_kk]
    a_rb = [jnp.where(incl, x, 0.0) for x in p_rb]
    a_rk = [jnp.where(incl, x, 0.0) for x in p_rk]
    st = [st_sc[i] for i in range(len(problems))]
    vs = _each(_stack, v)
    from_state = _each(lambda x, y, s: _mm_nt(cat(x, y), s), kh, rt, st)
    from_v = _each(lambda a1, a2, y: _mm(cat(a1, a2), y), a_kk, a_rk, vs)
    rhs = _each(lambda s, a: _stack(s[:c]) + a[:n], from_state, from_v)
    y0 = _each(lambda s, a: _stack(s[c:]) + a[n:], from_state, from_v)
    tinv = _unit_lower_inverses(lmat, c)
    u = _each(_mm, tinv, rhs)
    ys = _each(lambda y, a, x: y - _mm(a, x), y0, a_rb, u)
    vu = _each(lambda x, y: jnp.concatenate([x, y], axis=0), vs, u)
    kb = _each(lambda x, y: jnp.concatenate([_stack(x), -_stack(y)], axis=0), k_end, b_end)
    st_new = _each(lambda s, xe, x, y: s * jnp.exp(xe) + _mm(x.T, y), st, cs_end, vu, kb)
    for i, (g, p) in enumerate(problems):
        y_ref[g, 0, :, sls[p]] = ys[i][:c] + ys[i][c:]
        st_sc[i] = st_new[i]

    @pl.when(ci == pl.num_programs(1) - 1)
    def _():
        for i, (g, p) in enumerate(problems):
            st = st_sc[i]
            sout_ref[g, 2 * p] = st[:RW_HEAD, :RW_HEAD]
            sout_ref[g, 2 * p + 1] = st[RW_HEAD:, RW_HEAD:]


def _wkv_scan(r, lw, k, v, kap, b, s0, batch, seq, c, nseq=WKV_SEQS):
    nc = seq // c
    nseq = math.gcd(nseq, batch)
    tile = pl.BlockSpec((nseq, 1, c, RW_W), lambda bi, ci: (bi, ci, 0, 0))
    sspec = pl.BlockSpec((nseq, RW_HEADS, RW_HEAD, RW_HEAD), lambda bi, ci: (bi, 0, 0, 0))
    tri = jnp.tril(jnp.ones((c, c), F32)).astype(BF16)
    tiles = [t.reshape(batch, nc, c, RW_W) for t in (r, lw, k, v, kap, b)]
    y, s_new = pl.pallas_call(
        functools.partial(_wkv_scan_kernel, c=c, nseq=nseq),
        grid=(batch // nseq, nc),
        in_specs=[tile] * 6 + [sspec, _full((c, c))],
        out_specs=[tile, sspec],
        out_shape=[jax.ShapeDtypeStruct((batch, nc, c, RW_W), F32),
                   jax.ShapeDtypeStruct((batch, RW_HEADS, RW_HEAD, RW_HEAD), F32)],
        scratch_shapes=[pltpu.VMEM((nseq * RW_PAIRS, LANES, LANES), F32)],
        compiler_params=_cparams(("parallel", "arbitrary")),
        name="wkv_scan",
    )(*tiles, s0, tri)
    return y.reshape(batch * seq, RW_W), s_new


def _ffn_kernel(x_ref, att_ref, y_ref, bonus_ref, g_ref, lnw_ref, lnb_ref, seg_ref,
                woa_ref, wor_ref, gmix_ref, gpre_ref, gpost_ref, wg_ref, wu_ref, wd_ref, o_ref):
    y = y_ref[...]
    seg = seg_ref[...]
    mean = _seg_sum(y, seg) * (1.0 / RW_HEAD)
    d = y - mean
    var = _seg_sum(d * d, seg) * (1.0 / RW_HEAD)
    yn = d * lax.rsqrt(var + GN_EPS) * lnw_ref[...] + lnb_ref[...]
    rw_out = (yn + bonus_ref[...]) * g_ref[...]
    mix = _mm(att_ref[...], woa_ref[...]) + _mm(rw_out, wor_ref[...])
    x = x_ref[...] + _rms(mix, gmix_ref[...], RMS_EPS)
    f = _rms(x, gpre_ref[...], RMS_EPS).astype(BF16)
    dot = functools.partial(jnp.dot, preferred_element_type=F32)
    tff = D_FF // FFN_CHUNKS
    acc = None
    for j in range(FFN_CHUNKS):
        cs = slice(j * tff, (j + 1) * tff)
        gate = dot(f, wg_ref[:, cs])
        up = dot(f, wu_ref[:, cs])
        h = gate * _sigmoid(gate) * up
        part = dot(h.astype(BF16), wd_ref[cs, :])
        acc = part if acc is None else acc + part
    o_ref[...] = x + _rms(acc, gpost_ref[...], RMS_EPS)


def _mix_ffn(x, att, y, bonus, g, p, tm):
    m = x.shape[0]
    row = lambda w: pl.BlockSpec((tm, w), lambda i: (i, 0))
    vec = _const((1, RW_W))
    wide = _const((1, D_MODEL))
    return pl.pallas_call(
        _ffn_kernel,
        grid=(m // tm,),
        in_specs=[row(D_MODEL), row(ATT_W), row(RW_W), row(RW_W), row(RW_W), vec, vec,
                  _const((RW_W, RW_W)), _const((ATT_W, D_MODEL)), _const((RW_W, D_MODEL)),
                  wide, wide, wide,
                  _const((D_MODEL, D_FF)), _const((D_MODEL, D_FF)), _const((D_FF, D_MODEL))],
        out_specs=row(D_MODEL),
        out_shape=jax.ShapeDtypeStruct((m, D_MODEL), F32),
        compiler_params=_cparams(("parallel",)),
        name="mix_ffn",
    )(x, att, y, bonus, g, p['lnx_w'], p['lnx_b'], p['seg'], p['w_out_att'], p['w_out_rw'],
      p['norm_mix_post'], p['norm_ffn_pre'], p['norm_ffn_post'],
      p['ffn_gate'], p['ffn_up'], p['ffn_down'])


LANE_ROWS = 4


def _wkv_lanes_kernel(r_ref, lw_ref, k_ref, v_ref, kap_ref, b_ref, s_ref, y_ref, sout_ref,
                      xt_sc, y_sc, *, n_tok):
    batch = s_ref.shape[-1]
    for a, ref in enumerate((r_ref, lw_ref, k_ref, v_ref, kap_ref, b_ref)):
        for t in range(n_tok):
            x = ref[pl.ds(t, batch, stride=n_tok), :].T
            xt_sc[a, t] = jnp.exp(x) if a == 1 else x

    for hd in range(2):
        ch = slice(hd * RW_HEAD, (hd + 1) * RW_HEAD)

        def rows(blk, carry):
            for u in range(LANE_ROWS):
                i = blk * LANE_ROWS + u
                s = s_ref[hd, i]
                for t in range(n_tok):
                    v_i = xt_sc[3, t, pl.ds(hd * RW_HEAD + i, 1), :]
                    sk = jnp.sum(s * xt_sc[4, t, ch, :], axis=0, keepdims=True)
                    s = s * xt_sc[1, t, ch, :] - sk * xt_sc[5, t, ch, :] + v_i * xt_sc[2, t, ch, :]
                    y_sc[t, pl.ds(hd * RW_HEAD + i, 1), :] = jnp.sum(
                        s * xt_sc[0, t, ch, :], axis=0, keepdims=True)
                sout_ref[hd, i] = s
            return carry

        lax.fori_loop(0, RW_HEAD // LANE_ROWS, rows, 0)

    for t in range(n_tok):
        y_ref[pl.ds(t, batch, stride=n_tok), :] = y_sc[t].T


def _wkv_lanes(r, lw, k, v, kap, b, s0, batch, seq):
    st = jnp.transpose(s0, (1, 2, 3, 0))
    col = pl.BlockSpec((batch * seq, LANES), lambda p: (0, p))
    sspec = pl.BlockSpec((2, RW_HEAD, RW_HEAD, batch), lambda p: (p, 0, 0, 0))
    y, st_new = pl.pallas_call(
        functools.partial(_wkv_lanes_kernel, n_tok=seq),
        grid=(RW_PAIRS,),
        in_specs=[col] * 6 + [sspec],
        out_specs=[col, sspec],
        out_shape=[jax.ShapeDtypeStruct((batch * seq, RW_W), F32),
                   jax.ShapeDtypeStruct((RW_HEADS, RW_HEAD, RW_HEAD, batch), F32)],
        scratch_shapes=[pltpu.VMEM((6, seq, LANES, batch), F32), pltpu.VMEM((seq, LANES, batch), F32)],
        compiler_params=_cparams(("parallel",)),
        name="wkv_lanes",
    )(r, lw, k, v, kap, b, st)
    return y, jnp.transpose(st_new, (3, 0, 1, 2))


def _layer_params(l, w):
    row = lambda a: a[l].reshape(1, -1).astype(F32)
    zeros64 = jnp.zeros((64, RW_W), F32)
    head = jnp.arange(RW_W) // RW_HEAD
    return {
        'norm_mix_pre': row(w['norm_mix_pre']), 'norm_mix_post': row(w['norm_mix_post']),
        'norm_ffn_pre': row(w['norm_ffn_pre']), 'norm_ffn_post': row(w['norm_ffn_post']),
        'w_in': w['w_in'][l].astype(BF16),
        'w_out_att': w['w_out'][l][:ATT_W].astype(BF16),
        'w_out_rw': w['w_out'][l][ATT_W:].astype(BF16),
        'lam': [row(w[n]) for n in ('lambda_q1', 'lambda_k1', 'lambda_q2', 'lambda_k2')],
        'subln_w': row(w['subln_w']),
        'mu': row(w['rw_mu']), 'w0': row(w['rw_w0']), 'a0': row(w['rw_a0']),
        'kk': row(w['rw_kk']), 'ka': row(w['rw_ka']), 'rk': row(w['rw_rk']),
        'lnx_w': row(w['rw_lnx_w']), 'lnx_b': row(w['rw_lnx_b']),
        'w2pad': jnp.concatenate([w['rw_w2'][l], zeros64], axis=0).astype(BF16),
        'a2pad': jnp.concatenate([zeros64, w['rw_a2'][l]], axis=0).astype(BF16),
        'g2': w['rw_g2'][l].astype(BF16),
        'seg': (head[:, None] == head[None, :]).astype(BF16),
        'ffn_gate': w['ffn_gate'][l].astype(BF16), 'ffn_up': w['ffn_up'][l].astype(BF16),
        'ffn_down': w['ffn_down'][l].astype(BF16),
    }


def _layer(x, shift0, wkv0, attend, p, batch, seq):
    m = batch * seq
    if seq >= WKV_CHUNK:
        tm = min(ROW_TILE, seq)
        tiles = seq // tm
        first = jnp.zeros((batch, tiles, RW_PROJ_W), F32).at[:, 0, :].set(shift0).reshape(m // tm, 1, RW_PROJ_W)
        period, c = tm, WKV_CHUNK
    else:
        tm = min(ROW_TILE, m)
        tiles = 1
        first = jnp.zeros((batch, seq, RW_PROJ_W), F32).at[:, 0, :].set(shift0).reshape(m // tm, tm, RW_PROJ_W)
        period, c = seq, 8
    q, k, v, k4, v4, r, lw, k2, vr, kap, b, gate, bonus, tail = _inproj(x, first, period, tiles, p, tm)
    att = attend(q, k, v)
    scan_in = (r, lw, k2, vr, kap, b)
    if c > seq and batch % LANES == 0:
        y, wkv_new = _wkv_lanes(*scan_in, wkv0, batch, seq)
        shift_new = tail.reshape(batch, seq, RW_PROJ_W)[:, -1]
    elif c > seq:
        pad = lambda t: jnp.pad(t.reshape(batch, seq, RW_W), ((0, 0), (0, c - seq), (0, 0))).reshape(batch * c, RW_W)
        y, wkv_new = _wkv_scan(*[pad(t) for t in scan_in], wkv0, batch, c, c)
        y = y.reshape(batch, c, RW_W)[:, :seq].reshape(m, RW_W)
        shift_new = tail.reshape(batch, seq, RW_PROJ_W)[:, -1]
    else:
        y, wkv_new = _wkv_scan(*scan_in, wkv0, batch, seq, c)
        shift_new = tail.reshape(batch, tiles, RW_PROJ_W)[:, -1]
    out = _mix_ffn(x, att, y, bonus, gate, p, tm)
    return out, k4, v4, wkv_new, shift_new


def kernel(x_prompt, x_sample, cache_k, cache_v, state_wkv, state_shift, page_table, norm_mix_pre, norm_mix_post, norm_ffn_pre, norm_ffn_post, w_in, w_out, lambda_q1, lambda_k1, lambda_q2, lambda_k2, subln_w, rw_mu, rw_w0, rw_w2, rw_a0, rw_a2, rw_g2, rw_kk, rw_ka, rw_rk, rw_lnx_w, rw_lnx_b, ffn_gate, ffn_up, ffn_down):
    weights = dict(
        norm_mix_pre=norm_mix_pre, norm_mix_post=norm_mix_post, norm_ffn_pre=norm_ffn_pre,
        norm_ffn_post=norm_ffn_post, w_in=w_in, w_out=w_out, lambda_q1=lambda_q1,
        lambda_k1=lambda_k1, lambda_q2=lambda_q2, lambda_k2=lambda_k2, subln_w=subln_w,
        rw_mu=rw_mu, rw_w0=rw_w0, rw_w2=rw_w2, rw_a0=rw_a0, rw_a2=rw_a2, rw_g2=rw_g2,
        rw_kk=rw_kk, rw_ka=rw_ka, rw_rk=rw_rk, rw_lnx_w=rw_lnx_w, rw_lnx_b=rw_lnx_b,
        ffn_gate=ffn_gate, ffn_up=ffn_up, ffn_down=ffn_down)
    depth = w_in.shape[0]
    batch, seq, _ = x_prompt.shape
    db, dseq, _ = x_sample.shape
    n_pool, page = cache_k.shape[1], cache_k.shape[2]
    yp = x_prompt.reshape(batch * seq, D_MODEL)
    ys = x_sample.reshape(db * dseq, D_MODEL)
    outs = [[] for _ in range(8)]
    for l in range(depth):
        lam_init = 0.8 - 0.6 * math.exp(-0.3 * l)
        p = _layer_params(l, weights)
        ck = cache_k[l].reshape(n_pool, page * ATT_HEADS, LANES)
        cv = cache_v[l].reshape(n_pool, page * ATT_HEADS, LANES)

        def attend_p(q, k, v):
            return _attn_prompt(q, k, v, p['lam'], p['subln_w'], batch, seq, lam_init)

        def attend_s(q, k, v):
            t3 = lambda a: a.reshape(db, dseq, ATT_W)
            o = _attn_paged(t3(q), t3(k), t3(v), ck, cv, page_table, p['lam'], p['subln_w'], lam_init)
            return o.reshape(db * dseq, ATT_W)

        zshift = jnp.zeros((batch, RW_PROJ_W), F32)
        zwkv = jnp.zeros((batch, RW_HEADS, RW_HEAD, RW_HEAD), F32)
        yp, kp, vp, wp, sp = _layer(yp, zshift, zwkv, attend_p, p, batch, seq)
        ys, ks, vs, ws, ss = _layer(ys, state_shift[l], state_wkv[l], attend_s, p, db, dseq)
        kv_p = lambda a: a.reshape(batch, seq, ATT_HEADS, ATT_V_DIM)
        kv_s = lambda a: a.reshape(db, dseq, ATT_HEADS, ATT_V_DIM)
        for lst, val in zip(outs, (kv_p(kp), kv_p(vp), wp, sp, kv_s(ks), kv_s(vs), ws, ss)):
            lst.append(val)
    stacked = [jnp.stack(o) for o in outs]
    return (yp.reshape(batch, seq, D_MODEL), ys.reshape(db, dseq, D_MODEL), *stacked)
```

```python
import functools
import math

import jax
import jax.numpy as jnp
from jax import lax
from jax.experimental import pallas as pl
from jax.experimental.pallas import tpu as pltpu

F32 = jnp.float32
BF16 = jnp.bfloat16

D_MODEL = 1024
ATT_W = 512
RW_W = 512
ATT_HEADS = 4
ATT_QK_DIM = 64
ATT_V_DIM = 128
RW_HEAD = 64
RW_HEADS = 8
RW_PAIRS = RW_HEADS // 2
LORA_OFF = 3 * RW_W
GATE_OFF = LORA_OFF + 128
RW_PROJ_W = 1792
IN_W = 3 * ATT_W + RW_PROJ_W
D_FF = 2816
RMS_EPS = 1e-6
SUBLN_EPS = 1e-5
GN_EPS = 64e-5
NEG_BIG = -1e30

LANES = 128
VMEM_LIMIT = 56 * 1024 * 1024

ROW_TILE = 512
FFN_CHUNKS = 2
WKV_CHUNK = 64
WKV_SEQS = 4
INV_BASE = 8


def _cparams(sem):
    return pltpu.CompilerParams(dimension_semantics=sem, vmem_limit_bytes=VMEM_LIMIT)


def _full(shape):
    return pl.BlockSpec(shape, lambda *_: (0,) * len(shape))


def _mm(a, b):
    return jnp.dot(a.astype(BF16), b.astype(BF16), preferred_element_type=F32)


def _mm_nt(a, b):
    return lax.dot_general(a.astype(BF16), b.astype(BF16), (((1,), (1,)), ((), ())),
                           preferred_element_type=F32)


def _split2(x):
    hi = x.astype(BF16)
    lo = (x - hi.astype(F32)).astype(BF16)
    return hi, lo


def _seg_sum(a, seg_bf16):
    hi, lo = _split2(a)
    d = functools.partial(jnp.dot, preferred_element_type=F32)
    return d(hi, seg_bf16) + d(lo, seg_bf16)


def _const(shape):
    return pl.BlockSpec(shape, lambda *_: (0,) * len(shape), pipeline_mode=pl.Buffered(1))


def _mm_exact_lhs(a_bf16, b):
    hi, lo = _split2(b)
    d = functools.partial(jnp.dot, preferred_element_type=F32)
    return d(a_bf16, hi) + d(a_bf16, lo)


def _rms(x, g, eps):
    return x * lax.rsqrt(jnp.mean(x * x, axis=-1, keepdims=True) + eps) * g


def _diff_lambda(lq1, lk1, lq2, lk2, lam_init):
    s1 = jnp.sum(lq1[...] * lk1[...], axis=-1, keepdims=True)
    s2 = jnp.sum(lq2[...] * lk2[...], axis=-1, keepdims=True)
    return jnp.exp(s1) - jnp.exp(s2) + lam_init


def _softplus(x):
    return jnp.maximum(x, 0.0) + jnp.log1p(jnp.exp(-jnp.abs(x)))


def _sigmoid(x):
    return 1.0 / (1.0 + jnp.exp(-x))


def _inproj_kernel(x_ref, first_ref, g_ref, w_ref, mu_ref, w0_ref, a0_ref, kkw_ref, ka_ref, rk_ref,
                   w2_ref, a2_ref, g2_ref, seg_ref,
                   q_ref, k_ref, v_ref, k4_ref, v4_ref,
                   r_ref, lw_ref, k2_ref, vr_ref, kap_ref, b_ref, gate_ref, bonus_ref, tail_ref,
                   last_sc, *, period, tiles_per_seq):
    i = pl.program_id(0)
    tm = x_ref.shape[0]
    dot = functools.partial(jnp.dot, preferred_element_type=F32)
    carried = first_ref.shape[1] == 1
    halves = [slice(0, tm // 2), slice(tm // 2, tm)]
    xh = []
    for rs in halves:
        n = rs.stop - rs.start
        h = _rms(x_ref[rs, :], g_ref[...], RMS_EPS).astype(BF16)
        q_ref[rs, :] = dot(h, w_ref[:, 0:ATT_W])
        k = dot(h, w_ref[:, ATT_W:2 * ATT_W])
        v = dot(h, w_ref[:, 2 * ATT_W:3 * ATT_W])
        k_ref[rs, :] = k
        v_ref[rs, :] = v
        for hd in range(ATT_HEADS):
            sl = slice(hd * LANES, (hd + 1) * LANES)
            k4_ref[pl.ds(rs.start * ATT_HEADS + hd, n, stride=ATT_HEADS), :] = k[:, sl]
            v4_ref[pl.ds(rs.start * ATT_HEADS + hd, n, stride=ATT_HEADS), :] = v[:, sl]
        xh.append(dot(h, w_ref[:, 3 * ATT_W:]))

    if carried:
        @pl.when(i == 0)
        def _():
            last_sc[...] = jnp.zeros(last_sc.shape, F32)
        lead = jnp.where(i % tiles_per_seq == 0, first_ref[0], last_sc[...])
        last_sc[...] = xh[-1][-1:, :]
        tail_ref[0] = xh[-1][-1:, :]

    for idx, rs in enumerate(halves):
        x = xh[idx]
        row = lax.broadcasted_iota(jnp.int32, x.shape, 0)
        prev = pltpu.roll(x, 1, axis=0)
        if carried:
            prev = jnp.where(row == 0, lead if idx == 0 else xh[idx - 1][-1:, :], prev)
        else:
            prev = jnp.where(row % period == 0, first_ref[0, rs, :], prev)
            tail_ref[rs, :] = x
        xs = x + (prev - x) * mu_ref[...]
        r = xs[:, 0:RW_W]
        kx = xs[:, RW_W:2 * RW_W]
        vx = xs[:, 2 * RW_W:3 * RW_W]
        lo = xs[:, LORA_OFF:LORA_OFF + 128]
        g_lo = xs[:, GATE_OFF:GATE_OFF + 128]
        w_raw = -_softplus(-(w0_ref[...] + _mm(jnp.tanh(lo), w2_ref[...]))) - 0.5
        lw_ref[rs, :] = -jnp.exp(w_raw)
        a = _sigmoid(a0_ref[...] + _mm(lo, a2_ref[...]))
        gate_ref[rs, :] = _mm(_sigmoid(g_lo), g2_ref[...])
        kk = kx * kkw_ref[...]
        norm = jnp.sqrt(_seg_sum(kk * kk, seg_ref[...]))
        kap = kk / jnp.maximum(norm, 1e-12)
        k2 = kx * (1.0 + (a - 1.0) * ka_ref[...])
        r_ref[rs, :] = r
        k2_ref[rs, :] = k2
        vr_ref[rs, :] = vx
        kap_ref[rs, :] = kap
        b_ref[rs, :] = kap * a
        bonus_ref[rs, :] = _seg_sum(r * k2 * rk_ref[...], seg_ref[...]) * vx


def _inproj(x, first, period, tiles_per_seq, p, tm):
    m = x.shape[0]
    row = lambda w: pl.BlockSpec((tm, w), lambda i: (i, 0))
    heads_row = pl.BlockSpec((tm * ATT_HEADS, LANES), lambda i: (i, 0))
    vec = _const((1, RW_W))
    first_spec = pl.BlockSpec((1,) + first.shape[1:], lambda i: (i, 0, 0))
    wide = jax.ShapeDtypeStruct((m, RW_W), F32)
    if first.shape[1] == 1:
        tail_spec = pl.BlockSpec((1, 1, RW_PROJ_W), lambda i: (i, 0, 0))
        tail_shape = jax.ShapeDtypeStruct((m // tm, 1, RW_PROJ_W), F32)
    else:
        tail_spec = row(RW_PROJ_W)
        tail_shape = jax.ShapeDtypeStruct((m, RW_PROJ_W), F32)
    return pl.pallas_call(
        functools.partial(_inproj_kernel, period=period, tiles_per_seq=tiles_per_seq),
        grid=(m // tm,),
        in_specs=[row(D_MODEL), first_spec, _const((1, D_MODEL)), _const((D_MODEL, IN_W)),
                  _const((1, RW_PROJ_W)), vec, vec, vec, vec, vec,
                  _const((128, RW_W)), _const((128, RW_W)), _const((128, RW_W)), _const((RW_W, RW_W))],
        out_specs=[row(ATT_W)] * 3 + [heads_row] * 2 + [row(RW_W)] * 8 + [tail_spec],
        out_shape=[wide] * 3 + [jax.ShapeDtypeStruct((m * ATT_HEADS, LANES), F32)] * 2 + [wide] * 8
        + [tail_shape],
        scratch_shapes=[pltpu.VMEM((1, RW_PROJ_W), F32)],
        compiler_params=_cparams(("arbitrary",)),
        name="inproj",
    )(x, first, p['norm_mix_pre'], p['w_in'], p['mu'], p['w0'], p['a0'], p['kk'], p['ka'], p['rk'],
      p['w2pad'], p['a2pad'], p['g2'], p['seg'])


ATT_GRP = 128
ATT_ONES = 16
LOG2E = 1.4426950408889634


def _attn_prompt_kernel(lq1, lk1, lq2, lk2, sw_ref, q_ref, k_ref, v_ref, o_ref, kb_sc, vt_sc,
                        *, tq, lam_init):
    i = pl.program_id(2)
    groups = tq // ATT_GRP
    n_blk = k_ref.shape[0] // tq

    @pl.when(i == 0)
    def _():
        kb_sc[...] = k_ref[...].astype(BF16)
        ones = jnp.ones((ATT_ONES, tq), BF16)
        for jb in range(n_blk):
            vt_sc[jb] = jnp.concatenate([v_ref[jb * tq:(jb + 1) * tq, :].T.astype(BF16), ones], axis=0)

    lam = _diff_lambda(lq1, lk1, lq2, lk2, lam_init)
    lane = lax.broadcasted_iota(jnp.int32, (ATT_GRP, LANES), 1)
    q_maps = []
    for g in range(groups):
        q = q_ref[g * ATT_GRP:(g + 1) * ATT_GRP, :] * (ATT_QK_DIM ** -0.5 * LOG2E)
        q_maps.append(jnp.concatenate([jnp.where(lane < ATT_QK_DIM, q, 0.0),
                                       jnp.where(lane >= ATT_QK_DIM, q, 0.0)], axis=0).astype(BF16))

    def update(ss, vts, carry):
        ms, accs = carry
        m_new = _each(lambda m, s: jnp.maximum(m, jnp.max(s, axis=0, keepdims=True)), ms, ss)
        alpha = _each(lambda m, mn: jnp.exp2(m - mn), ms, m_new)
        ps = _each(lambda s, mn: jnp.exp2(s - mn).astype(BF16), ss, m_new)
        pv = _each(lambda vt, p: jnp.dot(vt, p, preferred_element_type=F32), vts, ps)
        accs = _each(lambda a, acc, x: a * acc + x, alpha, accs, pv)
        return tuple(m_new), tuple(accs)

    def full_block(j, carry):
        kb = kb_sc[pl.ds(pl.multiple_of(j * tq, tq), tq), :]
        vt = vt_sc[j]
        ss = [lax.dot_general(kb, qm, (((1,), (1,)), ((), ())), preferred_element_type=F32)
              for qm in q_maps]
        return update(ss, [vt] * len(q_maps), carry)

    w2 = 2 * ATT_GRP
    init = ((jnp.full((1, w2), NEG_BIG, F32),) * groups,
            (jnp.zeros((ATT_V_DIM + ATT_ONES, w2), F32),) * groups)
    carry = lax.fori_loop(0, i, full_block, init)

    start = pl.multiple_of(i * tq, tq)
    vt = vt_sc[i]
    ss, vts = [], []
    for g, qm in enumerate(q_maps):
        keys = (g + 1) * ATT_GRP
        kb = kb_sc[pl.ds(start, keys), :]
        s = lax.dot_general(kb, qm, (((1,), (1,)), ((), ())), preferred_element_type=F32)
        key = lax.broadcasted_iota(jnp.int32, (keys, w2), 0)
        qry = lax.broadcasted_iota(jnp.int32, (keys, w2), 1) % ATT_GRP + (keys - ATT_GRP)
        ss.append(jnp.where(key <= qry, s, -jnp.inf))
        vts.append(vt[:, :keys])
    _, accs = update(ss, vts, carry)
    for g in range(groups):
        z = accs[g][:ATT_V_DIM] / accs[g][ATT_V_DIM:ATT_V_DIM + 1]
        o = (z[:, :ATT_GRP] - lam * z[:, ATT_GRP:]).T
        o_ref[g * ATT_GRP:(g + 1) * ATT_GRP, :] = _rms(o, sw_ref[...], SUBLN_EPS) * (1.0 - lam_init)


def _attn_prompt(q, k, v, lam_params, subln_w, batch, seq, lam_init, tq=1024):
    tq = min(tq, seq)
    nq = seq // tq
    lam_specs = [_full((1, ATT_QK_DIM))] * 4
    qspec = pl.BlockSpec((tq, LANES), lambda b, h, i: (b * nq + i, h))
    kvspec = pl.BlockSpec((seq, LANES), lambda b, h, i: (b, h))
    return pl.pallas_call(
        functools.partial(_attn_prompt_kernel, tq=tq, lam_init=lam_init),
        grid=(batch, ATT_HEADS, nq),
        in_specs=lam_specs + [_full((1, ATT_V_DIM)), qspec, kvspec, kvspec],
        out_specs=qspec,
        out_shape=jax.ShapeDtypeStruct((batch * seq, ATT_W), F32),
        scratch_shapes=[pltpu.VMEM((seq, LANES), BF16),
                        pltpu.VMEM((nq, ATT_V_DIM + ATT_ONES, tq), BF16)],
        compiler_params=_cparams(("parallel", "parallel", "arbitrary")),
        name="attn_prompt",
    )(*lam_params, subln_w, q, k, v)


def _attn_paged_kernel(pt_ref, lq1, lk1, lq2, lk2, sw_ref, q_ref, kn_ref, vn_ref, *rest,
                       pages_per_step, n_tok, lam_init):
    del pt_ref
    k_refs = rest[:pages_per_step]
    v_refs = rest[pages_per_step:2 * pages_per_step]
    o_ref, m_sc, l_sc, acc_sc = rest[2 * pages_per_step:]
    c = pl.program_id(1)
    half = n_tok * ATT_HEADS
    rows = 2 * half
    page_rows = k_refs[0].shape[1]

    @pl.when(c == 0)
    def _():
        m_sc[...] = jnp.full(m_sc.shape, NEG_BIG, F32)
        l_sc[...] = jnp.zeros(l_sc.shape, F32)
        acc_sc[...] = jnp.zeros(acc_sc.shape, F32)

    row = lax.broadcasted_iota(jnp.int32, (rows, LANES), 0)
    lane = lax.broadcasted_iota(jnp.int32, (rows, LANES), 1)
    qs = jnp.where((lane // ATT_QK_DIM) == (row // half), q_ref[0], 0.0) * (ATT_QK_DIM ** -0.5)
    qs_b = qs.astype(BF16)
    srow = lax.broadcasted_iota(jnp.int32, (rows, page_rows), 0)
    scol = lax.broadcasted_iota(jnp.int32, (rows, page_rows), 1)
    same_head = (scol % ATT_HEADS) == (srow % ATT_HEADS)

    ss = [jnp.where(same_head, _mm_nt(qs_b, kr[0]), -jnp.inf) for kr in k_refs]
    m_old, l_old, acc_old = m_sc[...], l_sc[...], acc_sc[...]
    m = jnp.maximum(m_old, jnp.max(functools.reduce(jnp.maximum, ss), axis=-1, keepdims=True))
    alpha = jnp.exp(m_old - m)
    ps = [jnp.exp(s - m) for s in ss]
    l = alpha * l_old + jnp.sum(functools.reduce(jnp.add, ps), axis=-1, keepdims=True)
    pv = functools.reduce(jnp.add, [_mm(p, vr[0]) for p, vr in zip(ps, v_refs)])
    acc = alpha * acc_old + pv
    m_sc[...], l_sc[...], acc_sc[...] = m, l, acc

    @pl.when(c == pl.num_programs(1) - 1)
    def _():
        lam = _diff_lambda(lq1, lk1, lq2, lk2, lam_init)
        mm, ll, aa = m, l, acc
        r1 = lax.broadcasted_iota(jnp.int32, (rows, 1), 0)
        tok = (r1 // ATT_HEADS) % n_tok
        for j in range(n_tok):
            s = jnp.sum(qs * kn_ref[0, j], axis=-1, keepdims=True)
            s = jnp.where(tok >= j, s, -jnp.inf)
            m_new = jnp.maximum(mm, s)
            alpha = jnp.exp(mm - m_new)
            p = jnp.exp(s - m_new)
            ll = alpha * ll + p
            aa = alpha * aa + p * vn_ref[0, j]
            mm = m_new
        z = aa / ll
        o = z[:half] - lam * z[half:]
        o_ref[0] = _rms(o, sw_ref[...], SUBLN_EPS) * (1.0 - lam_init)


def _attn_paged(q, kn, vn, cache_k, cache_v, page_table, lam_params, subln_w, lam_init,
                pages_per_step=32):
    db, n_tok, _ = q.shape
    n_pages = page_table.shape[1]
    page_rows = cache_k.shape[1]
    pages_per_step = math.gcd(pages_per_step, n_pages)
    steps = n_pages // pages_per_step
    half = n_tok * ATT_HEADS
    rows = 2 * half
    pt_flat = page_table.reshape(-1)
    heads = lambda a: a.reshape(db, n_tok, ATT_HEADS, LANES)
    q_rows = jnp.broadcast_to(heads(q)[:, None], (db, 2, n_tok, ATT_HEADS, LANES)).reshape(db, rows, LANES)
    rep = lambda a: jnp.broadcast_to(
        heads(a)[:, :, None, None], (db, n_tok, 2, n_tok, ATT_HEADS, LANES)).reshape(db, n_tok, rows, LANES)

    def page_spec(i):
        return pl.BlockSpec(
            (1, page_rows, LANES),
            lambda b, c, pt: (pt[b * n_pages + c * pages_per_step + i], 0, 0))

    const = lambda shape: pl.BlockSpec(shape, lambda b, c, pt: (0,) * len(shape))
    new_spec = pl.BlockSpec((1, n_tok, rows, LANES), lambda b, c, pt: (b, 0, 0, 0))
    grid_spec = pltpu.PrefetchScalarGridSpec(
        num_scalar_prefetch=1,
        grid=(db, steps),
        in_specs=[const((1, ATT_QK_DIM))] * 4 + [const((1, ATT_V_DIM))]
        + [pl.BlockSpec((1, rows, LANES), lambda b, c, pt: (b, 0, 0)), new_spec, new_spec]
        + [page_spec(i) for i in range(pages_per_step)] * 2,
        out_specs=pl.BlockSpec((1, half, LANES), lambda b, c, pt: (b, 0, 0)),
        scratch_shapes=[pltpu.VMEM((rows, 1), F32), pltpu.VMEM((rows, 1), F32),
                        pltpu.VMEM((rows, LANES), F32)],
    )
    out = pl.pallas_call(
        functools.partial(_attn_paged_kernel, pages_per_step=pages_per_step, n_tok=n_tok,
                          lam_init=lam_init),
        grid_spec=grid_spec,
        out_shape=jax.ShapeDtypeStruct((db, half, LANES), F32),
        compiler_params=_cparams(("parallel", "arbitrary")),
        name="attn_paged",
    )(pt_flat, *lam_params, subln_w, q_rows, rep(kn), rep(vn),
      *([cache_k] * pages_per_step), *([cache_v] * pages_per_step))
    return out.reshape(db, n_tok, ATT_W)


def _stack(x):
    lane = lax.broadcasted_iota(jnp.int32, x.shape, 1)
    return jnp.concatenate([jnp.where(lane < RW_HEAD, x, 0.0),
                            jnp.where(lane >= RW_HEAD, x, 0.0)], axis=0)


def _each(fn, *lists):
    return [fn(*xs) for xs in zip(*lists)]


def _unit_lower_inverses(lmats, c):
    n = 2 * c
    ri = lax.broadcasted_iota(jnp.int32, (n, n), 0)
    ci = lax.broadcasted_iota(jnp.int32, (n, n), 1)
    eye = jnp.where(ri == ci, 1.0, 0.0)
    same = lambda s: (ri // s) == (ci // s)
    m1 = _each(lambda l: jnp.where(same(INV_BASE), -l, 0.0), lmats)
    m2 = _each(_mm, m1, m1)
    m4 = _each(_mm, m2, m2)
    t = _each(lambda a, b: eye + a + b + _mm(a, b), m1, m2)
    t = _each(lambda a, b: a + _mm(a, b), t, m4)
    s = INV_BASE
    while s < c:
        keep = same(2 * s) & jnp.logical_not(same(s))
        x = _each(lambda l, a: _mm(jnp.where(keep, l, 0.0), a), lmats, t)
        t = _each(lambda a, b: a - _mm(a, b), t, x)
        s *= 2
    return t


def _wkv_scan_kernel(r_ref, lw_ref, k_ref, v_ref, kap_ref, b_ref, s0_ref, tri_ref,
                     y_ref, sout_ref, st_sc, *, c, nseq):
    ci = pl.program_id(1)
    n = 2 * c
    problems = [(g, p) for g in range(nseq) for p in range(RW_PAIRS)]

    @pl.when(ci == 0)
    def _():
        zero = jnp.zeros((RW_HEAD, RW_HEAD), F32)
        for i, (g, p) in enumerate(problems):
            top = jnp.concatenate([s0_ref[g, 2 * p], zero], axis=1)
            bot = jnp.concatenate([zero, s0_ref[g, 2 * p + 1]], axis=1)
            st_sc[i] = jnp.concatenate([top, bot], axis=0)

    ri = lax.broadcasted_iota(jnp.int32, (n, n), 0)
    cj = lax.broadcasted_iota(jnp.int32, (n, n), 1)
    same_head = (ri // c) == (cj // c)
    strict = same_head & (ri > cj)
    incl = same_head & (ri >= cj)
    tri = tri_ref[...]

    sls = [slice(p * LANES, (p + 1) * LANES) for p in range(RW_PAIRS)]
    load = lambda ref: [ref[g, 0, :, sls[p]] for g, p in problems]
    lw, kap, r, k, b, v = (load(ref) for ref in (lw_ref, kap_ref, r_ref, k_ref, b_ref, v_ref))
    dup = lambda x: jnp.concatenate([x, x], axis=0)
    cs = _each(lambda x: _mm_exact_lhs(tri, x), lw)
    cs_end = [x[c - 1:c, :] for x in cs]
    kh = _each(lambda a, x, w: a * jnp.exp(x - w), kap, cs, lw)
    rt = _each(lambda a, x: a * jnp.exp(x), r, cs)
    e_out = [jnp.exp(-x) for x in cs]
    kt = _each(jnp.multiply, k, e_out)
    bt = _each(jnp.multiply, b, e_out)
    e_end = _each(lambda xe, x: jnp.exp(xe - x), cs_end, cs)
    k_end = _each(jnp.multiply, k, e_end)
    b_end = _each(jnp.multiply, b, e_end)

    khs, rts = _each(_stack, kh), _each(_stack, rt)
    bb, kk = _each(dup, bt), _each(dup, kt)
    cat = lambda x, y: jnp.concatenate([x, y], axis=0)
    if n % LANES == 0:
        prod = _each(lambda x1, x2, y1, y2: _mm_nt(cat(x1, x2), cat(y1, y2)), khs, rts, bb, kk)
        quad = lambda i, j: [x[i * n:(i + 1) * n, j * n:(j + 1) * n] for x in prod]
        p_kb, p_kk, p_rb, p_rk = quad(0, 0), quad(0, 1), quad(1, 0), quad(1, 1)
    else:
        p_kb, p_kk = _each(_mm_nt, khs, bb), _each(_mm_nt, khs, kk)
        p_rb, p_rk = _each(_mm_nt, rts, bb), _each(_mm_nt, rts, kk)
    lmat = [jnp.where(strict, x, 0.0) for x in p_kb]
    a_kk = [jnp.where(strict, x, 0.0) for x in p_kk]
    a_rb = [jnp.where(incl, x, 0.0) for x in p_rb]
    a_rk = [jnp.where(incl, x, 0.0) for x in p_rk]
    st = [st_sc[i] for i in range(len(problems))]
    vs = _each(_stack, v)
    from_state = _each(lambda x, y, s: _mm_nt(cat(x, y), s), kh, rt, st)
    from_v = _each(lambda a1, a2, y: _mm(cat(a1, a2), y), a_kk, a_rk, vs)
    rhs = _each(lambda s, a: _stack(s[:c]) + a[:n], from_state, from_v)
    y0 = _each(lambda s, a: _stack(s[c:]) + a[n:], from_state, from_v)
    tinv = _unit_lower_inverses(lmat, c)
    u = _each(_mm, tinv, rhs)
    ys = _each(lambda y, a, x: y - _mm(a, x), y0, a_rb, u)
    vu = _each(lambda x, y: jnp.concatenate([x, y], axis=0), vs, u)
    kb = _each(lambda x, y: jnp.concatenate([_stack(x), -_stack(y)], axis=0), k_end, b_end)
    st_new = _each(lambda s, xe, x, y: s * jnp.exp(xe) + _mm(x.T, y), st, cs_end, vu, kb)
    for i, (g, p) in enumerate(problems):
        y_ref[g, 0, :, sls[p]] = ys[i][:c] + ys[i][c:]
        st_sc[i] = st_new[i]

    @pl.when(ci == pl.num_programs(1) - 1)
    def _():
        for i, (g, p) in enumerate(problems):
            st = st_sc[i]
            sout_ref[g, 2 * p] = st[:RW_HEAD, :RW_HEAD]
            sout_ref[g, 2 * p + 1] = st[RW_HEAD:, RW_HEAD:]


def _wkv_scan(r, lw, k, v, kap, b, s0, batch, seq, c, nseq=WKV_SEQS):
    nc = seq // c
    nseq = math.gcd(nseq, batch)
    tile = pl.BlockSpec((nseq, 1, c, RW_W), lambda bi, ci: (bi, ci, 0, 0))
    sspec = pl.BlockSpec((nseq, RW_HEADS, RW_HEAD, RW_HEAD), lambda bi, ci: (bi, 0, 0, 0))
    tri = jnp.tril(jnp.ones((c, c), F32)).astype(BF16)
    tiles = [t.reshape(batch, nc, c, RW_W) for t in (r, lw, k, v, kap, b)]
    y, s_new = pl.pallas_call(
        functools.partial(_wkv_scan_kernel, c=c, nseq=nseq),
        grid=(batch // nseq, nc),
        in_specs=[tile] * 6 + [sspec, _full((c, c))],
        out_specs=[tile, sspec],
        out_shape=[jax.ShapeDtypeStruct((batch, nc, c, RW_W), F32),
                   jax.ShapeDtypeStruct((batch, RW_HEADS, RW_HEAD, RW_HEAD), F32)],
        scratch_shapes=[pltpu.VMEM((nseq * RW_PAIRS, LANES, LANES), F32)],
        compiler_params=_cparams(("parallel", "arbitrary")),
        name="wkv_scan",
    )(*tiles, s0, tri)
    return y.reshape(batch * seq, RW_W), s_new


def _ffn_kernel(x_ref, att_ref, y_ref, bonus_ref, g_ref, lnw_ref, lnb_ref, seg_ref,
                woa_ref, wor_ref, gmix_ref, gpre_ref, gpost_ref, wg_ref, wu_ref, wd_ref, o_ref):
    y = y_ref[...]
    seg = seg_ref[...]
    mean = _seg_sum(y, seg) * (1.0 / RW_HEAD)
    d = y - mean
    var = _seg_sum(d * d, seg) * (1.0 / RW_HEAD)
    yn = d * lax.rsqrt(var + GN_EPS) * lnw_ref[...] + lnb_ref[...]
    rw_out = (yn + bonus_ref[...]) * g_ref[...]
    mix = _mm(att_ref[...], woa_ref[...]) + _mm(rw_out, wor_ref[...])
    x = x_ref[...] + _rms(mix, gmix_ref[...], RMS_EPS)
    f = _rms(x, gpre_ref[...], RMS_EPS).astype(BF16)
    dot = functools.partial(jnp.dot, preferred_element_type=F32)
    tff = D_FF // FFN_CHUNKS
    acc = None
    for j in range(FFN_CHUNKS):
        cs = slice(j * tff, (j + 1) * tff)
        gate = dot(f, wg_ref[:, cs])
        up = dot(f, wu_ref[:, cs])
        h = gate * _sigmoid(gate) * up
        part = dot(h.astype(BF16), wd_ref[cs, :])
        acc = part if acc is None else acc + part
    o_ref[...] = x + _rms(acc, gpost_ref[...], RMS_EPS)


def _mix_ffn(x, att, y, bonus, g, p, tm):
    m = x.shape[0]
    row = lambda w: pl.BlockSpec((tm, w), lambda i: (i, 0))
    vec = _const((1, RW_W))
    wide = _const((1, D_MODEL))
    return pl.pallas_call(
        _ffn_kernel,
        grid=(m // tm,),
        in_specs=[row(D_MODEL), row(ATT_W), row(RW_W), row(RW_W), row(RW_W), vec, vec,
                  _const((RW_W, RW_W)), _const((ATT_W, D_MODEL)), _const((RW_W, D_MODEL)),
                  wide, wide, wide,
                  _const((D_MODEL, D_FF)), _const((D_MODEL, D_FF)), _const((D_FF, D_MODEL))],
        out_specs=row(D_MODEL),
        out_shape=jax.ShapeDtypeStruct((m, D_MODEL), F32),
        compiler_params=_cparams(("parallel",)),
        name="mix_ffn",
    )(x, att, y, bonus, g, p['lnx_w'], p['lnx_b'], p['seg'], p['w_out_att'], p['w_out_rw'],
      p['norm_mix_post'], p['norm_ffn_pre'], p['norm_ffn_post'],
      p['ffn_gate'], p['ffn_up'], p['ffn_down'])


LANE_ROWS = 4


def _wkv_lanes_kernel(r_ref, lw_ref, k_ref, v_ref, kap_ref, b_ref, s_ref, y_ref, sout_ref,
                      xt_sc, y_sc, *, n_tok):
    batch = s_ref.shape[-1]
    for a, ref in enumerate((r_ref, lw_ref, k_ref, v_ref, kap_ref, b_ref)):
        for t in range(n_tok):
            x = ref[pl.ds(t, batch, stride=n_tok), :].T
            xt_sc[a, t] = jnp.exp(x) if a == 1 else x

    for hd in range(2):
        ch = slice(hd * RW_HEAD, (hd + 1) * RW_HEAD)

        def rows(blk, carry):
            for u in range(LANE_ROWS):
                i = blk * LANE_ROWS + u
                s = s_ref[hd, i]
                for t in range(n_tok):
                    v_i = xt_sc[3, t, pl.ds(hd * RW_HEAD + i, 1), :]
                    sk = jnp.sum(s * xt_sc[4, t, ch, :], axis=0, keepdims=True)
                    s = s * xt_sc[1, t, ch, :] - sk * xt_sc[5, t, ch, :] + v_i * xt_sc[2, t, ch, :]
                    y_sc[t, pl.ds(hd * RW_HEAD + i, 1), :] = jnp.sum(
                        s * xt_sc[0, t, ch, :], axis=0, keepdims=True)
                sout_ref[hd, i] = s
            return carry

        lax.fori_loop(0, RW_HEAD // LANE_ROWS, rows, 0)

    for t in range(n_tok):
        y_ref[pl.ds(t, batch, stride=n_tok), :] = y_sc[t].T


def _wkv_lanes(r, lw, k, v, kap, b, s0, batch, seq):
    st = jnp.transpose(s0, (1, 2, 3, 0))
    col = pl.BlockSpec((batch * seq, LANES), lambda p: (0, p))
    sspec = pl.BlockSpec((2, RW_HEAD, RW_HEAD, batch), lambda p: (p, 0, 0, 0))
    y, st_new = pl.pallas_call(
        functools.partial(_wkv_lanes_kernel, n_tok=seq),
        grid=(RW_PAIRS,),
        in_specs=[col] * 6 + [sspec],
        out_specs=[col, sspec],
        out_shape=[jax.ShapeDtypeStruct((batch * seq, RW_W), F32),
                   jax.ShapeDtypeStruct((RW_HEADS, RW_HEAD, RW_HEAD, batch), F32)],
        scratch_shapes=[pltpu.VMEM((6, seq, LANES, batch), F32), pltpu.VMEM((seq, LANES, batch), F32)],
        compiler_params=_cparams(("parallel",)),
        name="wkv_lanes",
    )(r, lw, k, v, kap, b, st)
    return y, jnp.transpose(st_new, (3, 0, 1, 2))


def _layer_params(l, w):
    row = lambda a: a[l].reshape(1, -1).astype(F32)
    zeros64 = jnp.zeros((64, RW_W), F32)
    head = jnp.arange(RW_W) // RW_HEAD
    return {
        'norm_mix_pre': row(w['norm_mix_pre']), 'norm_mix_post': row(w['norm_mix_post']),
        'norm_ffn_pre': row(w['norm_ffn_pre']), 'norm_ffn_post': row(w['norm_ffn_post']),
        'w_in': w['w_in'][l].astype(BF16),
        'w_out_att': w['w_out'][l][:ATT_W].astype(BF16),
        'w_out_rw': w['w_out'][l][ATT_W:].astype(BF16),
        'lam': [row(w[n]) for n in ('lambda_q1', 'lambda_k1', 'lambda_q2', 'lambda_k2')],
        'subln_w': row(w['subln_w']),
        'mu': row(w['rw_mu']), 'w0': row(w['rw_w0']), 'a0': row(w['rw_a0']),
        'kk': row(w['rw_kk']), 'ka': row(w['rw_ka']), 'rk': row(w['rw_rk']),
        'lnx_w': row(w['rw_lnx_w']), 'lnx_b': row(w['rw_lnx_b']),
        'w2pad': jnp.concatenate([w['rw_w2'][l], zeros64], axis=0).astype(BF16),
        'a2pad': jnp.concatenate([zeros64, w['rw_a2'][l]], axis=0).astype(BF16),
        'g2': w['rw_g2'][l].astype(BF16),
        'seg': (head[:, None] == head[None, :]).astype(BF16),
        'ffn_gate': w['ffn_gate'][l].astype(BF16), 'ffn_up': w['ffn_up'][l].astype(BF16),
        'ffn_down': w['ffn_down'][l].astype(BF16),
    }


def _layer(x, shift0, wkv0, attend, p, batch, seq):
    m = batch * seq
    if seq >= WKV_CHUNK:
        tm = min(ROW_TILE, seq)
        tiles = seq // tm
        first = jnp.zeros((batch, tiles, RW_PROJ_W), F32).at[:, 0, :].set(shift0).reshape(m // tm, 1, RW_PROJ_W)
        period, c = tm, WKV_CHUNK
    else:
        tm = min(ROW_TILE, m)
        tiles = 1
        first = jnp.zeros((batch, seq, RW_PROJ_W), F32).at[:, 0, :].set(shift0).reshape(m // tm, tm, RW_PROJ_W)
        period, c = seq, 8
    q, k, v, k4, v4, r, lw, k2, vr, kap, b, gate, bonus, tail = _inproj(x, first, period, tiles, p, tm)
    att = attend(q, k, v)
    scan_in = (r, lw, k2, vr, kap, b)
    if c > seq and batch % LANES == 0:
        y, wkv_new = _wkv_lanes(*scan_in, wkv0, batch, seq)
        shift_new = tail.reshape(batch, seq, RW_PROJ_W)[:, -1]
    elif c > seq:
        pad = lambda t: jnp.pad(t.reshape(batch, seq, RW_W), ((0, 0), (0, c - seq), (0, 0))).reshape(batch * c, RW_W)
        y, wkv_new = _wkv_scan(*[pad(t) for t in scan_in], wkv0, batch, c, c)
        y = y.reshape(batch, c, RW_W)[:, :seq].reshape(m, RW_W)
        shift_new = tail.reshape(batch, seq, RW_PROJ_W)[:, -1]
    else:
        y, wkv_new = _wkv_scan(*scan_in, wkv0, batch, seq, c)
        shift_new = tail.reshape(batch, tiles, RW_PROJ_W)[:, -1]
    out = _mix_ffn(x, att, y, bonus, gate, p, tm)
    return out, k4, v4, wkv_new, shift_new


def kernel(x_prompt, x_sample, cache_k, cache_v, state_wkv, state_shift, page_table, norm_mix_pre, norm_mix_post, norm_ffn_pre, norm_ffn_post, w_in, w_out, lambda_q1, lambda_k1, lambda_q2, lambda_k2, subln_w, rw_mu, rw_w0, rw_w2, rw_a0, rw_a2, rw_g2, rw_kk, rw_ka, rw_rk, rw_lnx_w, rw_lnx_b, ffn_gate, ffn_up, ffn_down):
    weights = dict(
        norm_mix_pre=norm_mix_pre, norm_mix_post=norm_mix_post, norm_ffn_pre=norm_ffn_pre,
        norm_ffn_post=norm_ffn_post, w_in=w_in, w_out=w_out, lambda_q1=lambda_q1,
        lambda_k1=lambda_k1, lambda_q2=lambda_q2, lambda_k2=lambda_k2, subln_w=subln_w,
        rw_mu=rw_mu, rw_w0=rw_w0, rw_w2=rw_w2, rw_a0=rw_a0, rw_a2=rw_a2, rw_g2=rw_g2,
        rw_kk=rw_kk, rw_ka=rw_ka, rw_rk=rw_rk, rw_lnx_w=rw_lnx_w, rw_lnx_b=rw_lnx_b,
        ffn_gate=ffn_gate, ffn_up=ffn_up, ffn_down=ffn_down)
    depth = w_in.shape[0]
    batch, seq, _ = x_prompt.shape
    db, dseq, _ = x_sample.shape
    n_pool, page = cache_k.shape[1], cache_k.shape[2]
    yp = x_prompt.reshape(batch * seq, D_MODEL)
    ys = x_sample.reshape(db * dseq, D_MODEL)
    outs = [[] for _ in range(8)]
    for l in range(depth):
        lam_init = 0.8 - 0.6 * math.exp(-0.3 * l)
        p = _layer_params(l, weights)
        ck = cache_k[l].reshape(n_pool, page * ATT_HEADS, LANES)
        cv = cache_v[l].reshape(n_pool, page * ATT_HEADS, LANES)

        def attend_p(q, k, v):
            return _attn_prompt(q, k, v, p['lam'], p['subln_w'], batch, seq, lam_init)

        def attend_s(q, k, v):
            t3 = lambda a: a.reshape(db, dseq, ATT_W)
            o = _attn_paged(t3(q), t3(k), t3(v), ck, cv, page_table, p['lam'], p['subln_w'], lam_init)
            return o.reshape(db * dseq, ATT_W)

        zshift = jnp.zeros((batch, RW_PROJ_W), F32)
        zwkv = jnp.zeros((batch, RW_HEADS, RW_HEAD, RW_HEAD), F32)
        yp, kp, vp, wp, sp = _layer(yp, zshift, zwkv, attend_p, p, batch, seq)
        ys, ks, vs, ws, ss = _layer(ys, state_shift[l], state_wkv[l], attend_s, p, db, dseq)
        kv_p = lambda a: a.reshape(batch, seq, ATT_HEADS, ATT_V_DIM)
        kv_s = lambda a: a.reshape(db, dseq, ATT_HEADS, ATT_V_DIM)
        for lst, val in zip(outs, (kv_p(kp), kv_p(vp), wp, sp, kv_s(ks), kv_s(vs), ws, ss)):
            lst.append(val)
    stacked = [jnp.stack(o) for o in outs]
    return (yp.reshape(batch, seq, D_MODEL), ys.reshape(db, dseq, D_MODEL), *stacked)
```

```python
import functools
import math

import jax
import jax.numpy as jnp
from jax import lax
from jax.experimental import pallas as pl
from jax.experimental.pallas import tpu as pltpu

F32 = jnp.float32
BF16 = jnp.bfloat16

D_MODEL = 1024
ATT_W = 512
RW_W = 512
ATT_HEADS = 4
ATT_QK_DIM = 64
ATT_V_DIM = 128
RW_HEAD = 64
RW_HEADS = 8
RW_PAIRS = RW_HEADS // 2
LORA_OFF = 3 * RW_W
GATE_OFF = LORA_OFF + 128
RW_PROJ_W = 1792
IN_W = 3 * ATT_W + RW_PROJ_W
D_FF = 2816
RMS_EPS = 1e-6
SUBLN_EPS = 1e-5
GN_EPS = 64e-5
NEG_BIG = -1e30

LANES = 128
VMEM_LIMIT = 56 * 1024 * 1024

ROW_TILE = 512
FFN_CHUNKS = 2
WKV_CHUNK = 64
WKV_SEQS = 4
INV_BASE = 8


def _cparams(sem):
    return pltpu.CompilerParams(dimension_semantics=sem, vmem_limit_bytes=VMEM_LIMIT)


def _full(shape):
    return pl.BlockSpec(shape, lambda *_: (0,) * len(shape))


def _mm(a, b):
    return jnp.dot(a.astype(BF16), b.astype(BF16), preferred_element_type=F32)


def _mm_nt(a, b):
    return lax.dot_general(a.astype(BF16), b.astype(BF16), (((1,), (1,)), ((), ())),
                           preferred_element_type=F32)


def _split2(x):
    hi = x.astype(BF16)
    lo = (x - hi.astype(F32)).astype(BF16)
    return hi, lo


def _seg_sum(a, seg_bf16):
    hi, lo = _split2(a)
    d = functools.partial(jnp.dot, preferred_element_type=F32)
    return d(hi, seg_bf16) + d(lo, seg_bf16)


def _const(shape):
    return pl.BlockSpec(shape, lambda *_: (0,) * len(shape), pipeline_mode=pl.Buffered(1))


def _mm_exact_lhs(a_bf16, b):
    hi, lo = _split2(b)
    d = functools.partial(jnp.dot, preferred_element_type=F32)
    return d(a_bf16, hi) + d(a_bf16, lo)


def _rms(x, g, eps):
    return x * lax.rsqrt(jnp.mean(x * x, axis=-1, keepdims=True) + eps) * g


def _diff_lambda(lq1, lk1, lq2, lk2, lam_init):
    s1 = jnp.sum(lq1[...] * lk1[...], axis=-1, keepdims=True)
    s2 = jnp.sum(lq2[...] * lk2[...], axis=-1, keepdims=True)
    return jnp.exp(s1) - jnp.exp(s2) + lam_init


def _softplus(x):
    return jnp.maximum(x, 0.0) + jnp.log1p(jnp.exp(-jnp.abs(x)))


def _sigmoid(x):
    return 1.0 / (1.0 + jnp.exp(-x))


def _inproj_kernel(x_ref, first_ref, g_ref, w_ref, mu_ref, w0_ref, a0_ref, kkw_ref, ka_ref, rk_ref,
                   w2_ref, a2_ref, g2_ref, seg_ref,
                   q_ref, k_ref, v_ref, k4_ref, v4_ref,
                   r_ref, lw_ref, k2_ref, vr_ref, kap_ref, b_ref, gate_ref, bonus_ref, tail_ref,
                   last_sc, first_sc, *, period, tiles_per_seq, carried):
    i = pl.program_id(0)
    tm = x_ref.shape[0]
    dot = functools.partial(jnp.dot, preferred_element_type=F32)
    halves = [slice(0, tm // 2), slice(tm // 2, tm)]
    xh = []
    for rs in halves:
        n = rs.stop - rs.start
        h = _rms(x_ref[rs, :], g_ref[...], RMS_EPS).astype(BF16)
        q_ref[rs, :] = dot(h, w_ref[:, 0:ATT_W])
        k = dot(h, w_ref[:, ATT_W:2 * ATT_W])
        v = dot(h, w_ref[:, 2 * ATT_W:3 * ATT_W])
        k_ref[rs, :] = k
        v_ref[rs, :] = v
        for hd in range(ATT_HEADS):
            sl = slice(hd * LANES, (hd + 1) * LANES)
            k4_ref[pl.ds(rs.start * ATT_HEADS + hd, n, stride=ATT_HEADS), :] = k[:, sl]
            v4_ref[pl.ds(rs.start * ATT_HEADS + hd, n, stride=ATT_HEADS), :] = v[:, sl]
        xh.append(dot(h, w_ref[:, 3 * ATT_W:]))

    if carried:
        @pl.when(i == 0)
        def _():
            last_sc[...] = jnp.zeros(last_sc.shape, F32)
        lead = jnp.where(i % tiles_per_seq == 0, first_ref[0], last_sc[...])
        last_sc[...] = xh[-1][-1:, :]
        tail_ref[0] = xh[-1][-1:, :]
    else:
        first_sc[...] = jnp.zeros(first_sc.shape, F32)
        for c in range(first_sc.shape[0]):
            first_sc[c, pl.ds(0, tm // period, stride=period), :] = first_ref[0, :, c * LANES:(c + 1) * LANES]

    for idx, rs in enumerate(halves):
        x = xh[idx]
        row = lax.broadcasted_iota(jnp.int32, x.shape, 0)
        prev = pltpu.roll(x, 1, axis=0)
        if carried:
            prev = jnp.where(row == 0, lead if idx == 0 else xh[idx - 1][-1:, :], prev)
        else:
            first = jnp.concatenate([first_sc[c, rs, :] for c in range(first_sc.shape[0])], axis=1)
            prev = jnp.where(row % period == 0, first, prev)
            tail_ref[rs, :] = x
        xs = x + (prev - x) * mu_ref[...]
        r = xs[:, 0:RW_W]
        kx = xs[:, RW_W:2 * RW_W]
        vx = xs[:, 2 * RW_W:3 * RW_W]
        lo = xs[:, LORA_OFF:LORA_OFF + 128]
        g_lo = xs[:, GATE_OFF:GATE_OFF + 128]
        w_raw = -_softplus(-(w0_ref[...] + _mm(jnp.tanh(lo), w2_ref[...]))) - 0.5
        lw_ref[rs, :] = -jnp.exp(w_raw)
        a = _sigmoid(a0_ref[...] + _mm(lo, a2_ref[...]))
        gate_ref[rs, :] = _mm(_sigmoid(g_lo), g2_ref[...])
        kk = kx * kkw_ref[...]
        norm = jnp.sqrt(_seg_sum(kk * kk, seg_ref[...]))
        kap = kk / jnp.maximum(norm, 1e-12)
        k2 = kx * (1.0 + (a - 1.0) * ka_ref[...])
        r_ref[rs, :] = r
        k2_ref[rs, :] = k2
        vr_ref[rs, :] = vx
        kap_ref[rs, :] = kap
        b_ref[rs, :] = kap * a
        bonus_ref[rs, :] = _seg_sum(r * k2 * rk_ref[...], seg_ref[...]) * vx


def _inproj(x, first, period, tiles_per_seq, p, tm):
    m = x.shape[0]
    row = lambda w: pl.BlockSpec((tm, w), lambda i: (i, 0))
    heads_row = pl.BlockSpec((tm * ATT_HEADS, LANES), lambda i: (i, 0))
    vec = _const((1, RW_W))
    first_spec = pl.BlockSpec((1,) + first.shape[1:], lambda i: (i, 0, 0))
    wide = jax.ShapeDtypeStruct((m, RW_W), F32)
    carried = tiles_per_seq > 1 or period == tm
    if carried:
        tail_spec = pl.BlockSpec((1, 1, RW_PROJ_W), lambda i: (i, 0, 0))
        tail_shape = jax.ShapeDtypeStruct((m // tm, 1, RW_PROJ_W), F32)
    else:
        tail_spec = row(RW_PROJ_W)
        tail_shape = jax.ShapeDtypeStruct((m, RW_PROJ_W), F32)
    return pl.pallas_call(
        functools.partial(_inproj_kernel, period=period, tiles_per_seq=tiles_per_seq, carried=carried),
        grid=(m // tm,),
        in_specs=[row(D_MODEL), first_spec, _const((1, D_MODEL)), _const((D_MODEL, IN_W)),
                  _const((1, RW_PROJ_W)), vec, vec, vec, vec, vec,
                  _const((128, RW_W)), _const((128, RW_W)), _const((128, RW_W)), _const((RW_W, RW_W))],
        out_specs=[row(ATT_W)] * 3 + [heads_row] * 2 + [row(RW_W)] * 8 + [tail_spec],
        out_shape=[wide] * 3 + [jax.ShapeDtypeStruct((m * ATT_HEADS, LANES), F32)] * 2 + [wide] * 8
        + [tail_shape],
        scratch_shapes=[pltpu.VMEM((1, RW_PROJ_W), F32),
                        pltpu.VMEM((RW_PROJ_W // LANES, 8 if carried else tm, LANES), F32)],
        compiler_params=_cparams(("arbitrary",)),
        name="inproj",
    )(x, first, p['norm_mix_pre'], p['w_in'], p['mu'], p['w0'], p['a0'], p['kk'], p['ka'], p['rk'],
      p['w2pad'], p['a2pad'], p['g2'], p['seg'])


ATT_GRP = 128
ATT_ONES = 16
LOG2E = 1.4426950408889634


def _attn_prompt_kernel(lq1, lk1, lq2, lk2, sw_ref, q_ref, k_ref, v_ref, o_ref, kb_sc, vt_sc,
                        *, tq, lam_init):
    i = pl.program_id(2)
    groups = tq // ATT_GRP
    n_blk = k_ref.shape[0] // tq

    @pl.when(i == 0)
    def _():
        kb_sc[...] = k_ref[...].astype(BF16)
        ones = jnp.ones((ATT_ONES, tq), BF16)
        for jb in range(n_blk):
            vt_sc[jb] = jnp.concatenate([v_ref[jb * tq:(jb + 1) * tq, :].T.astype(BF16), ones], axis=0)

    lam = _diff_lambda(lq1, lk1, lq2, lk2, lam_init)
    lane = lax.broadcasted_iota(jnp.int32, (ATT_GRP, LANES), 1)
    q_maps = []
    for g in range(groups):
        q = q_ref[g * ATT_GRP:(g + 1) * ATT_GRP, :] * (ATT_QK_DIM ** -0.5 * LOG2E)
        q_maps.append(jnp.concatenate([jnp.where(lane < ATT_QK_DIM, q, 0.0),
                                       jnp.where(lane >= ATT_QK_DIM, q, 0.0)], axis=0).astype(BF16))

    def update(ss, vts, carry):
        ms, accs = carry
        m_new = _each(lambda m, s: jnp.maximum(m, jnp.max(s, axis=0, keepdims=True)), ms, ss)
        alpha = _each(lambda m, mn: jnp.exp2(m - mn), ms, m_new)
        ps = _each(lambda s, mn: jnp.exp2(s - mn).astype(BF16), ss, m_new)
        pv = _each(lambda vt, p: jnp.dot(vt, p, preferred_element_type=F32), vts, ps)
        accs = _each(lambda a, acc, x: a * acc + x, alpha, accs, pv)
        return tuple(m_new), tuple(accs)

    def full_block(j, carry):
        kb = kb_sc[pl.ds(pl.multiple_of(j * tq, tq), tq), :]
        vt = vt_sc[j]
        ss = [lax.dot_general(kb, qm, (((1,), (1,)), ((), ())), preferred_element_type=F32)
              for qm in q_maps]
        return update(ss, [vt] * len(q_maps), carry)

    w2 = 2 * ATT_GRP
    init = ((jnp.full((1, w2), NEG_BIG, F32),) * groups,
            (jnp.zeros((ATT_V_DIM + ATT_ONES, w2), F32),) * groups)
    carry = lax.fori_loop(0, i, full_block, init)

    start = pl.multiple_of(i * tq, tq)
    vt = vt_sc[i]
    ss, vts = [], []
    for g, qm in enumerate(q_maps):
        keys = (g + 1) * ATT_GRP
        kb = kb_sc[pl.ds(start, keys), :]
        s = lax.dot_general(kb, qm, (((1,), (1,)), ((), ())), preferred_element_type=F32)
        key = lax.broadcasted_iota(jnp.int32, (keys, w2), 0)
        qry = lax.broadcasted_iota(jnp.int32, (keys, w2), 1) % ATT_GRP + (keys - ATT_GRP)
        ss.append(jnp.where(key <= qry, s, -jnp.inf))
        vts.append(vt[:, :keys])
    _, accs = update(ss, vts, carry)
    for g in range(groups):
        z = accs[g][:ATT_V_DIM] / accs[g][ATT_V_DIM:ATT_V_DIM + 1]
        o = (z[:, :ATT_GRP] - lam * z[:, ATT_GRP:]).T
        o_ref[g * ATT_GRP:(g + 1) * ATT_GRP, :] = _rms(o, sw_ref[...], SUBLN_EPS) * (1.0 - lam_init)


def _attn_prompt(q, k, v, lam_params, subln_w, batch, seq, lam_init, tq=2048):
    tq = min(tq, seq)
    nq = seq // tq
    lam_specs = [_full((1, ATT_QK_DIM))] * 4
    qspec = pl.BlockSpec((tq, LANES), lambda b, h, i: (b * nq + i, h))
    kvspec = pl.BlockSpec((seq, LANES), lambda b, h, i: (b, h))
    return pl.pallas_call(
        functools.partial(_attn_prompt_kernel, tq=tq, lam_init=lam_init),
        grid=(batch, ATT_HEADS, nq),
        in_specs=lam_specs + [_full((1, ATT_V_DIM)), qspec, kvspec, kvspec],
        out_specs=qspec,
        out_shape=jax.ShapeDtypeStruct((batch * seq, ATT_W), F32),
        scratch_shapes=[pltpu.VMEM((seq, LANES), BF16),
                        pltpu.VMEM((nq, ATT_V_DIM + ATT_ONES, tq), BF16)],
        compiler_params=_cparams(("parallel", "parallel", "arbitrary")),
        name="attn_prompt",
    )(*lam_params, subln_w, q, k, v)


def _attn_paged_kernel(pt_ref, lq1, lk1, lq2, lk2, sw_ref, q_ref, kn_ref, vn_ref, *rest,
                       pages_per_step, n_tok, lam_init):
    del pt_ref
    k_refs = rest[:pages_per_step]
    v_refs = rest[pages_per_step:2 * pages_per_step]
    o_ref, m_sc, l_sc, acc_sc = rest[2 * pages_per_step:]
    c = pl.program_id(1)
    half = n_tok * ATT_HEADS
    rows = 2 * half
    page_rows = k_refs[0].shape[1]

    @pl.when(c == 0)
    def _():
        m_sc[...] = jnp.full(m_sc.shape, NEG_BIG, F32)
        l_sc[...] = jnp.zeros(l_sc.shape, F32)
        acc_sc[...] = jnp.zeros(acc_sc.shape, F32)

    row = lax.broadcasted_iota(jnp.int32, (rows, LANES), 0)
    lane = lax.broadcasted_iota(jnp.int32, (rows, LANES), 1)
    qs = jnp.where((lane // ATT_QK_DIM) == (row // half), q_ref[0], 0.0) * (ATT_QK_DIM ** -0.5)
    qs_b = qs.astype(BF16)
    srow = lax.broadcasted_iota(jnp.int32, (rows, page_rows), 0)
    scol = lax.broadcasted_iota(jnp.int32, (rows, page_rows), 1)
    same_head = (scol % ATT_HEADS) == (srow % ATT_HEADS)

    ss = [jnp.where(same_head, _mm_nt(qs_b, kr[0]), -jnp.inf) for kr in k_refs]
    m_old, l_old, acc_old = m_sc[...], l_sc[...], acc_sc[...]
    m = jnp.maximum(m_old, jnp.max(functools.reduce(jnp.maximum, ss), axis=-1, keepdims=True))
    alpha = jnp.exp(m_old - m)
    ps = [jnp.exp(s - m) for s in ss]
    l = alpha * l_old + jnp.sum(functools.reduce(jnp.add, ps), axis=-1, keepdims=True)
    pv = functools.reduce(jnp.add, [_mm(p, vr[0]) for p, vr in zip(ps, v_refs)])
    acc = alpha * acc_old + pv
    m_sc[...], l_sc[...], acc_sc[...] = m, l, acc

    @pl.when(c == pl.num_programs(1) - 1)
    def _():
        lam = _diff_lambda(lq1, lk1, lq2, lk2, lam_init)
        mm, ll, aa = m, l, acc
        r1 = lax.broadcasted_iota(jnp.int32, (rows, 1), 0)
        tok = (r1 // ATT_HEADS) % n_tok
        for j in range(n_tok):
            s = jnp.sum(qs * kn_ref[0, j], axis=-1, keepdims=True)
            s = jnp.where(tok >= j, s, -jnp.inf)
            m_new = jnp.maximum(mm, s)
            alpha = jnp.exp(mm - m_new)
            p = jnp.exp(s - m_new)
            ll = alpha * ll + p
            aa = alpha * aa + p * vn_ref[0, j]
            mm = m_new
        z = aa / ll
        o = z[:half] - lam * z[half:]
        o_ref[0] = _rms(o, sw_ref[...], SUBLN_EPS) * (1.0 - lam_init)


def _attn_paged(q, kn, vn, cache_k, cache_v, page_table, lam_params, subln_w, lam_init,
                pages_per_step=32):
    db, n_tok, _ = q.shape
    n_pages = page_table.shape[1]
    page_rows = cache_k.shape[1]
    pages_per_step = math.gcd(pages_per_step, n_pages)
    steps = n_pages // pages_per_step
    half = n_tok * ATT_HEADS
    rows = 2 * half
    pt_flat = page_table.reshape(-1)
    heads = lambda a: a.reshape(db, n_tok, ATT_HEADS, LANES)
    q_rows = jnp.broadcast_to(heads(q)[:, None], (db, 2, n_tok, ATT_HEADS, LANES)).reshape(db, rows, LANES)
    rep = lambda a: jnp.broadcast_to(
        heads(a)[:, :, None, None], (db, n_tok, 2, n_tok, ATT_HEADS, LANES)).reshape(db, n_tok, rows, LANES)

    def page_spec(i):
        return pl.BlockSpec(
            (1, page_rows, LANES),
            lambda b, c, pt: (pt[b * n_pages + c * pages_per_step + i], 0, 0))

    const = lambda shape: pl.BlockSpec(shape, lambda b, c, pt: (0,) * len(shape))
    new_spec = pl.BlockSpec((1, n_tok, rows, LANES), lambda b, c, pt: (b, 0, 0, 0))
    grid_spec = pltpu.PrefetchScalarGridSpec(
        num_scalar_prefetch=1,
        grid=(db, steps),
        in_specs=[const((1, ATT_QK_DIM))] * 4 + [const((1, ATT_V_DIM))]
        + [pl.BlockSpec((1, rows, LANES), lambda b, c, pt: (b, 0, 0)), new_spec, new_spec]
        + [page_spec(i) for i in range(pages_per_step)] * 2,
        out_specs=pl.BlockSpec((1, half, LANES), lambda b, c, pt: (b, 0, 0)),
        scratch_shapes=[pltpu.VMEM((rows, 1), F32), pltpu.VMEM((rows, 1), F32),
                        pltpu.VMEM((rows, LANES), F32)],
    )
    out = pl.pallas_call(
        functools.partial(_attn_paged_kernel, pages_per_step=pages_per_step, n_tok=n_tok,
                          lam_init=lam_init),
        grid_spec=grid_spec,
        out_shape=jax.ShapeDtypeStruct((db, half, LANES), F32),
        compiler_params=_cparams(("parallel", "arbitrary")),
        name="attn_paged",
    )(pt_flat, *lam_params, subln_w, q_rows, rep(kn), rep(vn),
      *([cache_k] * pages_per_step), *([cache_v] * pages_per_step))
    return out.reshape(db, n_tok, ATT_W)


def _stack(x):
    lane = lax.broadcasted_iota(jnp.int32, x.shape, 1)
    return jnp.concatenate([jnp.where(lane < RW_HEAD, x, 0.0),
                            jnp.where(lane >= RW_HEAD, x, 0.0)], axis=0)


def _each(fn, *lists):
    return [fn(*xs) for xs in zip(*lists)]


def _unit_lower_inverses(lmats, c):
    n = 2 * c
    ri = lax.broadcasted_iota(jnp.int32, (n, n), 0)
    ci = lax.broadcasted_iota(jnp.int32, (n, n), 1)
    eye = jnp.where(ri == ci, 1.0, 0.0)
    same = lambda s: (ri // s) == (ci // s)
    m1 = _each(lambda l: jnp.where(same(INV_BASE), -l, 0.0), lmats)
    m2 = _each(_mm, m1, m1)
    m4 = _each(_mm, m2, m2)
    t = _each(lambda a, b: eye + a + b + _mm(a, b), m1, m2)
    t = _each(lambda a, b: a + _mm(a, b), t, m4)
    s = INV_BASE
    while s < c:
        keep = same(2 * s) & jnp.logical_not(same(s))
        x = _each(lambda l, a: _mm(jnp.where(keep, l, 0.0), a), lmats, t)
        t = _each(lambda a, b: a - _mm(a, b), t, x)
        s *= 2
    return t


def _wkv_scan_kernel(r_ref, lw_ref, k_ref, v_ref, kap_ref, b_ref, s0_ref, tri_ref,
                     y_ref, sout_ref, st_sc, *, c, nseq):
    ci = pl.program_id(1)
    n = 2 * c
    problems = [(g, p) for g in range(nseq) for p in range(RW_PAIRS)]

    @pl.when(ci == 0)
    def _():
        zero = jnp.zeros((RW_HEAD, RW_HEAD), F32)
        for i, (g, p) in enumerate(problems):
            top = jnp.concatenate([s0_ref[g, 2 * p], zero], axis=1)
            bot = jnp.concatenate([zero, s0_ref[g, 2 * p + 1]], axis=1)
            st_sc[i] = jnp.concatenate([top, bot], axis=0)

    ri = lax.broadcasted_iota(jnp.int32, (n, n), 0)
    cj = lax.broadcasted_iota(jnp.int32, (n, n), 1)
    same_head = (ri // c) == (cj // c)
    strict = same_head & (ri > cj)
    incl = same_head & (ri >= cj)
    tri = tri_ref[...]

    sls = [slice(p * LANES, (p + 1) * LANES) for p in range(RW_PAIRS)]
    load = lambda ref: [ref[g, 0, :, sls[p]] for g, p in problems]
    lw, kap, r, k, b, v = (load(ref) for ref in (lw_ref, kap_ref, r_ref, k_ref, b_ref, v_ref))
    dup = lambda x: jnp.concatenate([x, x], axis=0)
    cs = _each(lambda x: _mm_exact_lhs(tri, x), lw)
    cs_end = [x[c - 1:c, :] for x in cs]
    kh = _each(lambda a, x, w: a * jnp.exp(x - w), kap, cs, lw)
    rt = _each(lambda a, x: a * jnp.exp(x), r, cs)
    e_out = [jnp.exp(-x) for x in cs]
    kt = _each(jnp.multiply, k, e_out)
    bt = _each(jnp.multiply, b, e_out)
    e_end = _each(lambda xe, x: jnp.exp(xe - x), cs_end, cs)
    k_end = _each(jnp.multiply, k, e_end)
    b_end = _each(jnp.multiply, b, e_end)

    khs, rts = _each(_stack, kh), _each(_stack, rt)
    bb, kk = _each(dup, bt), _each(dup, kt)
    cat = lambda x, y: jnp.concatenate([x, y], axis=0)
    if n % LANES == 0:
        prod = _each(lambda x1, x2, y1, y2: _mm_nt(cat(x1, x2), cat(y1, y2)), khs, rts, bb, kk)
        quad = lambda i, j: [x[i * n:(i + 1) * n, j * n:(j + 1) * n] for x in prod]
        p_kb, p_kk, p_rb, p_rk = quad(0, 0), quad(0, 1), quad(1, 0), quad(1, 1)
    else:
        p_kb, p_kk = _each(_mm_nt, khs, bb), _each(_mm_nt, khs, kk)
        p_rb, p_rk = _each(_mm_nt, rts, bb), _each(_mm_nt, rts, kk)
    lmat = [jnp.where(strict, x, 0.0) for x in p_kb]
    a_kk = [jnp.where(strict, x, 0.0) for x in p_kk]
    a_rb = [jnp.where(incl, x, 0.0) for x in p_rb]
    a_rk = [jnp.where(incl, x, 0.0) for x in p_rk]
    st = [st_sc[i] for i in range(len(problems))]
    vs = _each(_stack, v)
    from_state = _each(lambda x, y, s: _mm_nt(cat(x, y), s), kh, rt, st)
    from_v = _each(lambda a1, a2, y: _mm(cat(a1, a2), y), a_kk, a_rk, vs)
    rhs = _each(lambda s, a: _stack(s[:c]) + a[:n], from_state, from_v)
    y0 = _each(lambda s, a: _stack(s[c:]) + a[n:], from_state, from_v)
    tinv = _unit_lower_inverses(lmat, c)
    u = _each(_mm, tinv, rhs)
    ys = _each(lambda y, a, x: y - _mm(a, x), y0, a_rb, u)
    vu = _each(lambda x, y: jnp.concatenate([x, y], axis=0), vs, u)
    kb = _each(lambda x, y: jnp.concatenate([_stack(x), -_stack(y)], axis=0), k_end, b_end)
    st_new = _each(lambda s, xe, x, y: s * jnp.exp(xe) + _mm(x.T, y), st, cs_end, vu, kb)
    for i, (g, p) in enumerate(problems):
        y_ref[g, 0, :, sls[p]] = ys[i][:c] + ys[i][c:]
        st_sc[i] = st_new[i]

    @pl.when(ci == pl.num_programs(1) - 1)
    def _():
        for i, (g, p) in enumerate(problems):
            st = st_sc[i]
            sout_ref[g, 2 * p] = st[:RW_HEAD, :RW_HEAD]
            sout_ref[g, 2 * p + 1] = st[RW_HEAD:, RW_HEAD:]


def _wkv_scan(r, lw, k, v, kap, b, s0, batch, seq, c, nseq=WKV_SEQS):
    nc = seq // c
    nseq = math.gcd(nseq, batch)
    tile = pl.BlockSpec((nseq, 1, c, RW_W), lambda bi, ci: (bi, ci, 0, 0))
    sspec = pl.BlockSpec((nseq, RW_HEADS, RW_HEAD, RW_HEAD), lambda bi, ci: (bi, 0, 0, 0))
    tri = jnp.tril(jnp.ones((c, c), F32)).astype(BF16)
    tiles = [t.reshape(batch, nc, c, RW_W) for t in (r, lw, k, v, kap, b)]
    y, s_new = pl.pallas_call(
        functools.partial(_wkv_scan_kernel, c=c, nseq=nseq),
        grid=(batch // nseq, nc),
        in_specs=[tile] * 6 + [sspec, _full((c, c))],
        out_specs=[tile, sspec],
        out_shape=[jax.ShapeDtypeStruct((batch, nc, c, RW_W), F32),
                   jax.ShapeDtypeStruct((batch, RW_HEADS, RW_HEAD, RW_HEAD), F32)],
        scratch_shapes=[pltpu.VMEM((nseq * RW_PAIRS, LANES, LANES), F32)],
        compiler_params=_cparams(("parallel", "arbitrary")),
        name="wkv_scan",
    )(*tiles, s0, tri)
    return y.reshape(batch * seq, RW_W), s_new


def _ffn_kernel(x_ref, att_ref, y_ref, bonus_ref, g_ref, lnw_ref, lnb_ref, seg_ref,
                woa_ref, wor_ref, gmix_ref, gpre_ref, gpost_ref, wg_ref, wu_ref, wd_ref, o_ref):
    y = y_ref[...]
    seg = seg_ref[...]
    mean = _seg_sum(y, seg) * (1.0 / RW_HEAD)
    d = y - mean
    var = _seg_sum(d * d, seg) * (1.0 / RW_HEAD)
    yn = d * lax.rsqrt(var + GN_EPS) * lnw_ref[...] + lnb_ref[...]
    rw_out = (yn + bonus_ref[...]) * g_ref[...]
    mix = _mm(att_ref[...], woa_ref[...]) + _mm(rw_out, wor_ref[...])
    x = x_ref[...] + _rms(mix, gmix_ref[...], RMS_EPS)
    f = _rms(x, gpre_ref[...], RMS_EPS).astype(BF16)
    dot = functools.partial(jnp.dot, preferred_element_type=F32)
    tff = D_FF // FFN_CHUNKS
    acc = None
    for j in range(FFN_CHUNKS):
        cs = slice(j * tff, (j + 1) * tff)
        gate = dot(f, wg_ref[:, cs])
        up = dot(f, wu_ref[:, cs])
        h = gate * _sigmoid(gate) * up
        part = dot(h.astype(BF16), wd_ref[cs, :])
        acc = part if acc is None else acc + part
    o_ref[...] = x + _rms(acc, gpost_ref[...], RMS_EPS)


def _mix_ffn(x, att, y, bonus, g, p, tm):
    m = x.shape[0]
    row = lambda w: pl.BlockSpec((tm, w), lambda i: (i, 0))
    vec = _const((1, RW_W))
    wide = _const((1, D_MODEL))
    return pl.pallas_call(
        _ffn_kernel,
        grid=(m // tm,),
        in_specs=[row(D_MODEL), row(ATT_W), row(RW_W), row(RW_W), row(RW_W), vec, vec,
                  _const((RW_W, RW_W)), _const((ATT_W, D_MODEL)), _const((RW_W, D_MODEL)),
                  wide, wide, wide,
                  _const((D_MODEL, D_FF)), _const((D_MODEL, D_FF)), _const((D_FF, D_MODEL))],
        out_specs=row(D_MODEL),
        out_shape=jax.ShapeDtypeStruct((m, D_MODEL), F32),
        compiler_params=_cparams(("parallel",)),
        name="mix_ffn",
    )(x, att, y, bonus, g, p['lnx_w'], p['lnx_b'], p['seg'], p['w_out_att'], p['w_out_rw'],
      p['norm_mix_post'], p['norm_ffn_pre'], p['norm_ffn_post'],
      p['ffn_gate'], p['ffn_up'], p['ffn_down'])


LANE_ROWS = 8


def _wkv_lanes_kernel(r_ref, lw_ref, k_ref, v_ref, kap_ref, b_ref, s_ref, y_ref, sout_ref,
                      xt_sc, y_sc, *, n_tok):
    batch = s_ref.shape[-1]
    for a, ref in enumerate((r_ref, lw_ref, k_ref, v_ref, kap_ref, b_ref)):
        for t in range(n_tok):
            x = ref[pl.ds(t, batch, stride=n_tok), :].T
            xt_sc[a, t] = jnp.exp(x) if a == 1 else x

    for hd in range(2):
        ch = slice(hd * RW_HEAD, (hd + 1) * RW_HEAD)

        def rows(blk, carry):
            for u in range(LANE_ROWS):
                i = blk * LANE_ROWS + u
                s = s_ref[hd, i]
                for t in range(n_tok):
                    v_i = xt_sc[3, t, pl.ds(hd * RW_HEAD + i, 1), :]
                    sk = jnp.sum(s * xt_sc[4, t, ch, :], axis=0, keepdims=True)
                    s = s * xt_sc[1, t, ch, :] - sk * xt_sc[5, t, ch, :] + v_i * xt_sc[2, t, ch, :]
                    y_sc[t, pl.ds(hd * RW_HEAD + i, 1), :] = jnp.sum(
                        s * xt_sc[0, t, ch, :], axis=0, keepdims=True)
                sout_ref[hd, i] = s
            return carry

        lax.fori_loop(0, RW_HEAD // LANE_ROWS, rows, 0)

    for t in range(n_tok):
        y_ref[pl.ds(t, batch, stride=n_tok), :] = y_sc[t].T


def _wkv_lanes(r, lw, k, v, kap, b, s0, batch, seq):
    st = jnp.transpose(s0, (1, 2, 3, 0))
    col = pl.BlockSpec((batch * seq, LANES), lambda p: (0, p))
    sspec = pl.BlockSpec((2, RW_HEAD, RW_HEAD, batch), lambda p: (p, 0, 0, 0))
    y, st_new = pl.pallas_call(
        functools.partial(_wkv_lanes_kernel, n_tok=seq),
        grid=(RW_PAIRS,),
        in_specs=[col] * 6 + [sspec],
        out_specs=[col, sspec],
        out_shape=[jax.ShapeDtypeStruct((batch * seq, RW_W), F32),
                   jax.ShapeDtypeStruct((RW_HEADS, RW_HEAD, RW_HEAD, batch), F32)],
        scratch_shapes=[pltpu.VMEM((6, seq, LANES, batch), F32), pltpu.VMEM((seq, LANES, batch), F32)],
        compiler_params=_cparams(("parallel",)),
        name="wkv_lanes",
    )(r, lw, k, v, kap, b, st)
    return y, jnp.transpose(st_new, (3, 0, 1, 2))


def _layer_params(l, w):
    row = lambda a: a[l].reshape(1, -1).astype(F32)
    zeros64 = jnp.zeros((64, RW_W), F32)
    head = jnp.arange(RW_W) // RW_HEAD
    return {
        'norm_mix_pre': row(w['norm_mix_pre']), 'norm_mix_post': row(w['norm_mix_post']),
        'norm_ffn_pre': row(w['norm_ffn_pre']), 'norm_ffn_post': row(w['norm_ffn_post']),
        'w_in': w['w_in'][l].astype(BF16),
        'w_out_att': w['w_out'][l][:ATT_W].astype(BF16),
        'w_out_rw': w['w_out'][l][ATT_W:].astype(BF16),
        'lam': [row(w[n]) for n in ('lambda_q1', 'lambda_k1', 'lambda_q2', 'lambda_k2')],
        'subln_w': row(w['subln_w']),
        'mu': row(w['rw_mu']), 'w0': row(w['rw_w0']), 'a0': row(w['rw_a0']),
        'kk': row(w['rw_kk']), 'ka': row(w['rw_ka']), 'rk': row(w['rw_rk']),
        'lnx_w': row(w['rw_lnx_w']), 'lnx_b': row(w['rw_lnx_b']),
        'w2pad': jnp.concatenate([w['rw_w2'][l], zeros64], axis=0).astype(BF16),
        'a2pad': jnp.concatenate([zeros64, w['rw_a2'][l]], axis=0).astype(BF16),
        'g2': w['rw_g2'][l].astype(BF16),
        'seg': (head[:, None] == head[None, :]).astype(BF16),
        'ffn_gate': w['ffn_gate'][l].astype(BF16), 'ffn_up': w['ffn_up'][l].astype(BF16),
        'ffn_down': w['ffn_down'][l].astype(BF16),
    }


def _layer(x, shift0, wkv0, attend, p, batch, seq):
    m = batch * seq
    if seq >= WKV_CHUNK:
        tm = min(ROW_TILE, seq)
        tiles = seq // tm
        first = jnp.zeros((batch, tiles, RW_PROJ_W), F32).at[:, 0, :].set(shift0).reshape(m // tm, 1, RW_PROJ_W)
        period, c = tm, WKV_CHUNK
    else:
        tm = min(ROW_TILE, m)
        tiles = 1
        first = shift0.reshape(m // tm, tm // seq, RW_PROJ_W)
        period, c = seq, 8
    q, k, v, k4, v4, r, lw, k2, vr, kap, b, gate, bonus, tail = _inproj(x, first, period, tiles, p, tm)
    att = attend(q, k, v)
    scan_in = (r, lw, k2, vr, kap, b)
    if c > seq and batch % LANES == 0:
        y, wkv_new = _wkv_lanes(*scan_in, wkv0, batch, seq)
        shift_new = tail[seq - 1::seq]
    elif c > seq:
        pad = lambda t: jnp.pad(t.reshape(batch, seq, RW_W), ((0, 0), (0, c - seq), (0, 0))).reshape(batch * c, RW_W)
        y, wkv_new = _wkv_scan(*[pad(t) for t in scan_in], wkv0, batch, c, c)
        y = y.reshape(batch, c, RW_W)[:, :seq].reshape(m, RW_W)
        shift_new = tail.reshape(batch, seq, RW_PROJ_W)[:, -1]
    else:
        y, wkv_new = _wkv_scan(*scan_in, wkv0, batch, seq, c)
        shift_new = tail.reshape(batch, tiles, RW_PROJ_W)[:, -1]
    out = _mix_ffn(x, att, y, bonus, gate, p, tm)
    return out, k4, v4, wkv_new, shift_new


def kernel(x_prompt, x_sample, cache_k, cache_v, state_wkv, state_shift, page_table, norm_mix_pre, norm_mix_post, norm_ffn_pre, norm_ffn_post, w_in, w_out, lambda_q1, lambda_k1, lambda_q2, lambda_k2, subln_w, rw_mu, rw_w0, rw_w2, rw_a0, rw_a2, rw_g2, rw_kk, rw_ka, rw_rk, rw_lnx_w, rw_lnx_b, ffn_gate, ffn_up, ffn_down):
    weights = dict(
        norm_mix_pre=norm_mix_pre, norm_mix_post=norm_mix_post, norm_ffn_pre=norm_ffn_pre,
        norm_ffn_post=norm_ffn_post, w_in=w_in, w_out=w_out, lambda_q1=lambda_q1,
        lambda_k1=lambda_k1, lambda_q2=lambda_q2, lambda_k2=lambda_k2, subln_w=subln_w,
        rw_mu=rw_mu, rw_w0=rw_w0, rw_w2=rw_w2, rw_a0=rw_a0, rw_a2=rw_a2, rw_g2=rw_g2,
        rw_kk=rw_kk, rw_ka=rw_ka, rw_rk=rw_rk, rw_lnx_w=rw_lnx_w, rw_lnx_b=rw_lnx_b,
        ffn_gate=ffn_gate, ffn_up=ffn_up, ffn_down=ffn_down)
    depth = w_in.shape[0]
    batch, seq, _ = x_prompt.shape
    db, dseq, _ = x_sample.shape
    n_pool, page = cache_k.shape[1], cache_k.shape[2]
    yp = x_prompt.reshape(batch * seq, D_MODEL)
    ys = x_sample.reshape(db * dseq, D_MODEL)
    outs = [[] for _ in range(8)]
    for l in range(depth):
        lam_init = 0.8 - 0.6 * math.exp(-0.3 * l)
        p = _layer_params(l, weights)
        ck = cache_k[l].reshape(n_pool, page * ATT_HEADS, LANES)
        cv = cache_v[l].reshape(n_pool, page * ATT_HEADS, LANES)

        def attend_p(q, k, v):
            return _attn_prompt(q, k, v, p['lam'], p['subln_w'], batch, seq, lam_init)

        def attend_s(q, k, v):
            t3 = lambda a: a.reshape(db, dseq, ATT_W)
            o = _attn_paged(t3(q), t3(k), t3(v), ck, cv, page_table, p['lam'], p['subln_w'], lam_init)
            return o.reshape(db * dseq, ATT_W)

        zshift = jnp.zeros((batch, RW_PROJ_W), F32)
        zwkv = jnp.zeros((batch, RW_HEADS, RW_HEAD, RW_HEAD), F32)
        yp, kp, vp, wp, sp = _layer(yp, zshift, zwkv, attend_p, p, batch, seq)
        ys, ks, vs, ws, ss = _layer(ys, state_shift[l], state_wkv[l], attend_s, p, db, dseq)
        kv_p = lambda a: a.reshape(batch, seq, ATT_HEADS, ATT_V_DIM)
        kv_s = lambda a: a.reshape(db, dseq, ATT_HEADS, ATT_V_DIM)
        for lst, val in zip(outs, (kv_p(kp), kv_p(vp), wp, sp, kv_s(ks), kv_s(vs), ws, ss)):
            lst.append(val)
    stacked = [jnp.stack(o) for o in outs]
    return (yp.reshape(batch, seq, D_MODEL), ys.reshape(db, dseq, D_MODEL), *stacked)
```

```python
import functools
import math

import jax
import jax.numpy as jnp
from jax import lax
from jax.experimental import pallas as pl
from jax.experimental.pallas import tpu as pltpu

F32 = jnp.float32
BF16 = jnp.bfloat16

D_MODEL = 1024
ATT_W = 512
RW_W = 512
ATT_HEADS = 4
ATT_QK_DIM = 64
ATT_V_DIM = 128
RW_HEAD = 64
RW_HEADS = 8
RW_PAIRS = RW_HEADS // 2
LORA_OFF = 3 * RW_W
GATE_OFF = LORA_OFF + 128
RW_PROJ_W = 1792
IN_W = 3 * ATT_W + RW_PROJ_W
D_FF = 2816
RMS_EPS = 1e-6
SUBLN_EPS = 1e-5
GN_EPS = 64e-5
NEG_BIG = -1e30

LANES = 128
VMEM_LIMIT = 56 * 1024 * 1024

ROW_TILE = 512
FFN_CHUNKS = 1
WKV_CHUNK = 64
WKV_SEQS = 4
INV_BASE = 8


def _cparams(sem):
    return pltpu.CompilerParams(dimension_semantics=sem, vmem_limit_bytes=VMEM_LIMIT)


def _full(shape):
    return pl.BlockSpec(shape, lambda *_: (0,) * len(shape))


def _mm(a, b):
    return jnp.dot(a.astype(BF16), b.astype(BF16), preferred_element_type=F32)


def _mm_nt(a, b):
    return lax.dot_general(a.astype(BF16), b.astype(BF16), (((1,), (1,)), ((), ())),
                           preferred_element_type=F32)


def _split2(x):
    hi = x.astype(BF16)
    lo = (x - hi.astype(F32)).astype(BF16)
    return hi, lo


def _seg_sum(a, seg_bf16):
    hi, lo = _split2(a)
    d = functools.partial(jnp.dot, preferred_element_type=F32)
    return d(hi, seg_bf16) + d(lo, seg_bf16)


def _const(shape):
    return pl.BlockSpec(shape, lambda *_: (0,) * len(shape), pipeline_mode=pl.Buffered(1))


def _mm_exact_lhs(a_bf16, b):
    hi, lo = _split2(b)
    d = functools.partial(jnp.dot, preferred_element_type=F32)
    return d(a_bf16, hi) + d(a_bf16, lo)


def _rms(x, g, eps):
    return x * lax.rsqrt(jnp.mean(x * x, axis=-1, keepdims=True) + eps) * g


def _diff_lambda(lq1, lk1, lq2, lk2, lam_init):
    s1 = jnp.sum(lq1[...] * lk1[...], axis=-1, keepdims=True)
    s2 = jnp.sum(lq2[...] * lk2[...], axis=-1, keepdims=True)
    return jnp.exp(s1) - jnp.exp(s2) + lam_init


def _softplus(x):
    return jnp.maximum(x, 0.0) + jnp.log1p(jnp.exp(-jnp.abs(x)))


def _sigmoid(x):
    return 1.0 / (1.0 + jnp.exp(-x))


def _inproj_kernel(x_ref, first_ref, g_ref, w_ref, mu_ref, w0_ref, a0_ref, kkw_ref, ka_ref, rk_ref,
                   w2_ref, a2_ref, g2_ref, seg_ref,
                   q_ref, k_ref, v_ref, k4_ref, v4_ref,
                   r_ref, lw_ref, k2_ref, vr_ref, kap_ref, b_ref, gate_ref, bonus_ref, tail_ref,
                   last_sc, first_sc, *, period, tiles_per_seq, carried):
    i = pl.program_id(0)
    tm = x_ref.shape[0]
    dot = functools.partial(jnp.dot, preferred_element_type=F32)
    halves = [slice(0, tm // 2), slice(tm // 2, tm)]
    xh = []
    for rs in halves:
        n = rs.stop - rs.start
        h = _rms(x_ref[rs, :], g_ref[...], RMS_EPS).astype(BF16)
        q_ref[rs, :] = dot(h, w_ref[:, 0:ATT_W])
        k = dot(h, w_ref[:, ATT_W:2 * ATT_W])
        v = dot(h, w_ref[:, 2 * ATT_W:3 * ATT_W])
        k_ref[rs, :] = k
        v_ref[rs, :] = v
        for hd in range(ATT_HEADS):
            sl = slice(hd * LANES, (hd + 1) * LANES)
            k4_ref[pl.ds(rs.start * ATT_HEADS + hd, n, stride=ATT_HEADS), :] = k[:, sl]
            v4_ref[pl.ds(rs.start * ATT_HEADS + hd, n, stride=ATT_HEADS), :] = v[:, sl]
        xh.append(dot(h, w_ref[:, 3 * ATT_W:]))

    if carried:
        @pl.when(i == 0)
        def _():
            last_sc[...] = jnp.zeros(last_sc.shape, F32)
        lead = jnp.where(i % tiles_per_seq == 0, first_ref[0], last_sc[...])
        last_sc[...] = xh[-1][-1:, :]
        tail_ref[0] = xh[-1][-1:, :]
    else:
        first_sc[...] = jnp.zeros(first_sc.shape, F32)
        for c in range(first_sc.shape[0]):
            first_sc[c, pl.ds(0, tm // period, stride=period), :] = first_ref[0, :, c * LANES:(c + 1) * LANES]

    for idx, rs in enumerate(halves):
        x = xh[idx]
        row = lax.broadcasted_iota(jnp.int32, x.shape, 0)
        prev = pltpu.roll(x, 1, axis=0)
        if carried:
            prev = jnp.where(row == 0, lead if idx == 0 else xh[idx - 1][-1:, :], prev)
        else:
            first = jnp.concatenate([first_sc[c, rs, :] for c in range(first_sc.shape[0])], axis=1)
            prev = jnp.where(row % period == 0, first, prev)
            tail_ref[rs, :] = x
        xs = x + (prev - x) * mu_ref[...]
        r = xs[:, 0:RW_W]
        kx = xs[:, RW_W:2 * RW_W]
        vx = xs[:, 2 * RW_W:3 * RW_W]
        lo = xs[:, LORA_OFF:LORA_OFF + 128]
        g_lo = xs[:, GATE_OFF:GATE_OFF + 128]
        w_raw = -_softplus(-(w0_ref[...] + _mm(jnp.tanh(lo), w2_ref[...]))) - 0.5
        lw_ref[rs, :] = -jnp.exp(w_raw)
        a = _sigmoid(a0_ref[...] + _mm(lo, a2_ref[...]))
        gate_ref[rs, :] = _mm(_sigmoid(g_lo), g2_ref[...])
        kk = kx * kkw_ref[...]
        norm = jnp.sqrt(_seg_sum(kk * kk, seg_ref[...]))
        kap = kk / jnp.maximum(norm, 1e-12)
        k2 = kx * (1.0 + (a - 1.0) * ka_ref[...])
        r_ref[rs, :] = r
        k2_ref[rs, :] = k2
        vr_ref[rs, :] = vx
        kap_ref[rs, :] = kap
        b_ref[rs, :] = kap * a
        bonus_ref[rs, :] = _seg_sum(r * k2 * rk_ref[...], seg_ref[...]) * vx


def _inproj(x, first, period, tiles_per_seq, p, tm):
    m = x.shape[0]
    row = lambda w: pl.BlockSpec((tm, w), lambda i: (i, 0))
    heads_row = pl.BlockSpec((tm * ATT_HEADS, LANES), lambda i: (i, 0))
    vec = _const((1, RW_W))
    first_spec = pl.BlockSpec((1,) + first.shape[1:], lambda i: (i, 0, 0))
    wide = jax.ShapeDtypeStruct((m, RW_W), F32)
    carried = tiles_per_seq > 1 or period == tm
    if carried:
        tail_spec = pl.BlockSpec((1, 1, RW_PROJ_W), lambda i: (i, 0, 0))
        tail_shape = jax.ShapeDtypeStruct((m // tm, 1, RW_PROJ_W), F32)
    else:
        tail_spec = row(RW_PROJ_W)
        tail_shape = jax.ShapeDtypeStruct((m, RW_PROJ_W), F32)
    return pl.pallas_call(
        functools.partial(_inproj_kernel, period=period, tiles_per_seq=tiles_per_seq, carried=carried),
        grid=(m // tm,),
        in_specs=[row(D_MODEL), first_spec, _const((1, D_MODEL)), _const((D_MODEL, IN_W)),
                  _const((1, RW_PROJ_W)), vec, vec, vec, vec, vec,
                  _const((128, RW_W)), _const((128, RW_W)), _const((128, RW_W)), _const((RW_W, RW_W))],
        out_specs=[row(ATT_W)] * 3 + [heads_row] * 2 + [row(RW_W)] * 8 + [tail_spec],
        out_shape=[wide] * 3 + [jax.ShapeDtypeStruct((m * ATT_HEADS, LANES), F32)] * 2 + [wide] * 8
        + [tail_shape],
        scratch_shapes=[pltpu.VMEM((1, RW_PROJ_W), F32),
                        pltpu.VMEM((RW_PROJ_W // LANES, 8 if carried else tm, LANES), F32)],
        compiler_params=_cparams(("arbitrary",)),
        name="inproj",
    )(x, first, p['norm_mix_pre'], p['w_in'], p['mu'], p['w0'], p['a0'], p['kk'], p['ka'], p['rk'],
      p['w2pad'], p['a2pad'], p['g2'], p['seg'])


ATT_GRP = 128
ATT_ONES = 16
LOG2E = 1.4426950408889634


def _attn_prompt_kernel(lq1, lk1, lq2, lk2, sw_ref, q_ref, k_ref, v_ref, o_ref, kb_sc, vt_sc,
                        *, tq, lam_init):
    i = pl.program_id(2)
    groups = tq // ATT_GRP
    n_blk = k_ref.shape[0] // tq

    @pl.when(i == 0)
    def _():
        kb_sc[...] = k_ref[...].astype(BF16)
        ones = jnp.ones((ATT_ONES, tq), BF16)
        for jb in range(n_blk):
            vt_sc[jb] = jnp.concatenate([v_ref[jb * tq:(jb + 1) * tq, :].T.astype(BF16), ones], axis=0)

    lam = _diff_lambda(lq1, lk1, lq2, lk2, lam_init)
    lane = lax.broadcasted_iota(jnp.int32, (ATT_GRP, LANES), 1)
    q_maps = []
    for g in range(groups):
        q = q_ref[g * ATT_GRP:(g + 1) * ATT_GRP, :] * (ATT_QK_DIM ** -0.5 * LOG2E)
        q_maps.append(jnp.concatenate([jnp.where(lane < ATT_QK_DIM, q, 0.0),
                                       jnp.where(lane >= ATT_QK_DIM, q, 0.0)], axis=0).astype(BF16))

    def update(ss, vts, carry):
        ms, accs = carry
        m_new = _each(lambda m, s: jnp.maximum(m, jnp.max(s, axis=0, keepdims=True)), ms, ss)
        alpha = _each(lambda m, mn: jnp.exp2(m - mn), ms, m_new)
        ps = _each(lambda s, mn: jnp.exp2(s - mn).astype(BF16), ss, m_new)
        pv = _each(lambda vt, p: jnp.dot(vt, p, preferred_element_type=F32), vts, ps)
        accs = _each(lambda a, acc, x: a * acc + x, alpha, accs, pv)
        return tuple(m_new), tuple(accs)

    def full_block(j, carry):
        kb = kb_sc[pl.ds(pl.multiple_of(j * tq, tq), tq), :]
        vt = vt_sc[j]
        ss = [lax.dot_general(kb, qm, (((1,), (1,)), ((), ())), preferred_element_type=F32)
              for qm in q_maps]
        return update(ss, [vt] * len(q_maps), carry)

    w2 = 2 * ATT_GRP
    init = ((jnp.full((1, w2), NEG_BIG, F32),) * groups,
            (jnp.zeros((ATT_V_DIM + ATT_ONES, w2), F32),) * groups)
    carry = lax.fori_loop(0, i, full_block, init)

    start = pl.multiple_of(i * tq, tq)
    vt = vt_sc[i]
    ss, vts = [], []
    for g, qm in enumerate(q_maps):
        keys = (g + 1) * ATT_GRP
        kb = kb_sc[pl.ds(start, keys), :]
        s = lax.dot_general(kb, qm, (((1,), (1,)), ((), ())), preferred_element_type=F32)
        key = lax.broadcasted_iota(jnp.int32, (keys, w2), 0)
        qry = lax.broadcasted_iota(jnp.int32, (keys, w2), 1) % ATT_GRP + (keys - ATT_GRP)
        ss.append(jnp.where(key <= qry, s, -jnp.inf))
        vts.append(vt[:, :keys])
    _, accs = update(ss, vts, carry)
    for g in range(groups):
        z = accs[g][:ATT_V_DIM] / accs[g][ATT_V_DIM:ATT_V_DIM + 1]
        o = (z[:, :ATT_GRP] - lam * z[:, ATT_GRP:]).T
        o_ref[g * ATT_GRP:(g + 1) * ATT_GRP, :] = _rms(o, sw_ref[...], SUBLN_EPS) * (1.0 - lam_init)


def _attn_prompt(q, k, v, lam_params, subln_w, batch, seq, lam_init, tq=2048):
    tq = min(tq, seq)
    nq = seq // tq
    lam_specs = [_full((1, ATT_QK_DIM))] * 4
    qspec = pl.BlockSpec((tq, LANES), lambda b, h, i: (b * nq + i, h))
    kvspec = pl.BlockSpec((seq, LANES), lambda b, h, i: (b, h))
    return pl.pallas_call(
        functools.partial(_attn_prompt_kernel, tq=tq, lam_init=lam_init),
        grid=(batch, ATT_HEADS, nq),
        in_specs=lam_specs + [_full((1, ATT_V_DIM)), qspec, kvspec, kvspec],
        out_specs=qspec,
        out_shape=jax.ShapeDtypeStruct((batch * seq, ATT_W), F32),
        scratch_shapes=[pltpu.VMEM((seq, LANES), BF16),
                        pltpu.VMEM((nq, ATT_V_DIM + ATT_ONES, tq), BF16)],
        compiler_params=_cparams(("parallel", "parallel", "arbitrary")),
        name="attn_prompt",
    )(*lam_params, subln_w, q, k, v)


def _attn_paged_kernel(pt_ref, lq1, lk1, lq2, lk2, sw_ref, q_ref, kn_ref, vn_ref, *rest,
                       pages_per_step, n_tok, lam_init):
    del pt_ref
    k_refs = rest[:pages_per_step]
    v_refs = rest[pages_per_step:2 * pages_per_step]
    o_ref, m_sc, l_sc, acc_sc = rest[2 * pages_per_step:]
    c = pl.program_id(1)
    half = n_tok * ATT_HEADS
    rows = 2 * half
    page_rows = k_refs[0].shape[1]

    @pl.when(c == 0)
    def _():
        m_sc[...] = jnp.full(m_sc.shape, NEG_BIG, F32)
        l_sc[...] = jnp.zeros(l_sc.shape, F32)
        acc_sc[...] = jnp.zeros(acc_sc.shape, F32)

    row = lax.broadcasted_iota(jnp.int32, (rows, LANES), 0)
    lane = lax.broadcasted_iota(jnp.int32, (rows, LANES), 1)
    qs = jnp.where((lane // ATT_QK_DIM) == (row // half), q_ref[0], 0.0) * (ATT_QK_DIM ** -0.5)
    qs_b = qs.astype(BF16)
    srow = lax.broadcasted_iota(jnp.int32, (rows, page_rows), 0)
    scol = lax.broadcasted_iota(jnp.int32, (rows, page_rows), 1)
    same_head = (scol % ATT_HEADS) == (srow % ATT_HEADS)

    ss = [jnp.where(same_head, _mm_nt(qs_b, kr[0]), -jnp.inf) for kr in k_refs]
    m_old, l_old, acc_old = m_sc[...], l_sc[...], acc_sc[...]
    m = jnp.maximum(m_old, jnp.max(functools.reduce(jnp.maximum, ss), axis=-1, keepdims=True))
    alpha = jnp.exp(m_old - m)
    ps = [jnp.exp(s - m) for s in ss]
    l = alpha * l_old + jnp.sum(functools.reduce(jnp.add, ps), axis=-1, keepdims=True)
    pv = functools.reduce(jnp.add, [_mm(p, vr[0]) for p, vr in zip(ps, v_refs)])
    acc = alpha * acc_old + pv
    m_sc[...], l_sc[...], acc_sc[...] = m, l, acc

    @pl.when(c == pl.num_programs(1) - 1)
    def _():
        lam = _diff_lambda(lq1, lk1, lq2, lk2, lam_init)
        mm, ll, aa = m, l, acc
        r1 = lax.broadcasted_iota(jnp.int32, (rows, 1), 0)
        tok = (r1 // ATT_HEADS) % n_tok
        for j in range(n_tok):
            s = jnp.sum(qs * kn_ref[0, j], axis=-1, keepdims=True)
            s = jnp.where(tok >= j, s, -jnp.inf)
            m_new = jnp.maximum(mm, s)
            alpha = jnp.exp(mm - m_new)
            p = jnp.exp(s - m_new)
            ll = alpha * ll + p
            aa = alpha * aa + p * vn_ref[0, j]
            mm = m_new
        z = aa / ll
        o = z[:half] - lam * z[half:]
        o_ref[0] = _rms(o, sw_ref[...], SUBLN_EPS) * (1.0 - lam_init)


def _attn_paged(q, kn, vn, cache_k, cache_v, page_table, lam_params, subln_w, lam_init,
                pages_per_step=32):
    db, n_tok, _ = q.shape
    n_pages = page_table.shape[1]
    page_rows = cache_k.shape[1]
    pages_per_step = math.gcd(pages_per_step, n_pages)
    steps = n_pages // pages_per_step
    half = n_tok * ATT_HEADS
    rows = 2 * half
    pt_flat = page_table.reshape(-1)
    heads = lambda a: a.reshape(db, n_tok, ATT_HEADS, LANES)
    q_rows = jnp.broadcast_to(heads(q)[:, None], (db, 2, n_tok, ATT_HEADS, LANES)).reshape(db, rows, LANES)
    rep = lambda a: jnp.broadcast_to(
        heads(a)[:, :, None, None], (db, n_tok, 2, n_tok, ATT_HEADS, LANES)).reshape(db, n_tok, rows, LANES)

    def page_spec(i):
        return pl.BlockSpec(
            (1, page_rows, LANES),
            lambda b, c, pt: (pt[b * n_pages + c * pages_per_step + i], 0, 0))

    const = lambda shape: pl.BlockSpec(shape, lambda b, c, pt: (0,) * len(shape))
    new_spec = pl.BlockSpec((1, n_tok, rows, LANES), lambda b, c, pt: (b, 0, 0, 0))
    grid_spec = pltpu.PrefetchScalarGridSpec(
        num_scalar_prefetch=1,
        grid=(db, steps),
        in_specs=[const((1, ATT_QK_DIM))] * 4 + [const((1, ATT_V_DIM))]
        + [pl.BlockSpec((1, rows, LANES), lambda b, c, pt: (b, 0, 0)), new_spec, new_spec]
        + [page_spec(i) for i in range(pages_per_step)] * 2,
        out_specs=pl.BlockSpec((1, half, LANES), lambda b, c, pt: (b, 0, 0)),
        scratch_shapes=[pltpu.VMEM((rows, 1), F32), pltpu.VMEM((rows, 1), F32),
                        pltpu.VMEM((rows, LANES), F32)],
    )
    out = pl.pallas_call(
        functools.partial(_attn_paged_kernel, pages_per_step=pages_per_step, n_tok=n_tok,
                          lam_init=lam_init),
        grid_spec=grid_spec,
        out_shape=jax.ShapeDtypeStruct((db, half, LANES), F32),
        compiler_params=_cparams(("parallel", "arbitrary")),
        name="attn_paged",
    )(pt_flat, *lam_params, subln_w, q_rows, rep(kn), rep(vn),
      *([cache_k] * pages_per_step), *([cache_v] * pages_per_step))
    return out.reshape(db, n_tok, ATT_W)


def _stack(x):
    lane = lax.broadcasted_iota(jnp.int32, x.shape, 1)
    return jnp.concatenate([jnp.where(lane < RW_HEAD, x, 0.0),
                            jnp.where(lane >= RW_HEAD, x, 0.0)], axis=0)


def _each(fn, *lists):
    return [fn(*xs) for xs in zip(*lists)]


def _unit_lower_inverses(lmats, c):
    n = 2 * c
    ri = lax.broadcasted_iota(jnp.int32, (n, n), 0)
    ci = lax.broadcasted_iota(jnp.int32, (n, n), 1)
    eye = jnp.where(ri == ci, 1.0, 0.0)
    same = lambda s: (ri // s) == (ci // s)
    m1 = _each(lambda l: jnp.where(same(INV_BASE), -l, 0.0), lmats)
    m2 = _each(_mm, m1, m1)
    m4 = _each(_mm, m2, m2)
    t = _each(lambda a, b: eye + a + b + _mm(a, b), m1, m2)
    t = _each(lambda a, b: a + _mm(a, b), t, m4)
    s = INV_BASE
    while s < c:
        keep = same(2 * s) & jnp.logical_not(same(s))
        x = _each(lambda l, a: _mm(jnp.where(keep, l, 0.0), a), lmats, t)
        t = _each(lambda a, b: a - _mm(a, b), t, x)
        s *= 2
    return t


def _wkv_scan_kernel(r_ref, lw_ref, k_ref, v_ref, kap_ref, b_ref, s0_ref, tri_ref,
                     y_ref, sout_ref, st_sc, *, c, nseq):
    ci = pl.program_id(1)
    n = 2 * c
    problems = [(g, p) for g in range(nseq) for p in range(RW_PAIRS)]

    @pl.when(ci == 0)
    def _():
        zero = jnp.zeros((RW_HEAD, RW_HEAD), F32)
        for i, (g, p) in enumerate(problems):
            top = jnp.concatenate([s0_ref[g, 2 * p], zero], axis=1)
            bot = jnp.concatenate([zero, s0_ref[g, 2 * p + 1]], axis=1)
            st_sc[i] = jnp.concatenate([top, bot], axis=0)

    ri = lax.broadcasted_iota(jnp.int32, (n, n), 0)
    cj = lax.broadcasted_iota(jnp.int32, (n, n), 1)
    same_head = (ri // c) == (cj // c)
    strict = same_head & (ri > cj)
    incl = same_head & (ri >= cj)
    tri = tri_ref[...]

    sls = [slice(p * LANES, (p + 1) * LANES) for p in range(RW_PAIRS)]
    load = lambda ref: [ref[g, 0, :, sls[p]] for g, p in problems]
    lw, kap, r, k, b, v = (load(ref) for ref in (lw_ref, kap_ref, r_ref, k_ref, b_ref, v_ref))
    dup = lambda x: jnp.concatenate([x, x], axis=0)
    cs = _each(lambda x: _mm_exact_lhs(tri, x), lw)
    cs_end = [x[c - 1:c, :] for x in cs]
    kh = _each(lambda a, x, w: a * jnp.exp(x - w), kap, cs, lw)
    rt = _each(lambda a, x: a * jnp.exp(x), r, cs)
    e_out = [jnp.exp(-x) for x in cs]
    kt = _each(jnp.multiply, k, e_out)
    bt = _each(jnp.multiply, b, e_out)
    e_end = _each(lambda xe, x: jnp.exp(xe - x), cs_end, cs)
    k_end = _each(jnp.multiply, k, e_end)
    b_end = _each(jnp.multiply, b, e_end)

    khs, rts = _each(_stack, kh), _each(_stack, rt)
    bb, kk = _each(dup, bt), _each(dup, kt)
    cat = lambda x, y: jnp.concatenate([x, y], axis=0)
    if n % LANES == 0:
        prod = _each(lambda x1, x2, y1, y2: _mm_nt(cat(x1, x2), cat(y1, y2)), khs, rts, bb, kk)
        quad = lambda i, j: [x[i * n:(i + 1) * n, j * n:(j + 1) * n] for x in prod]
        p_kb, p_kk, p_rb, p_rk = quad(0, 0), quad(0, 1), quad(1, 0), quad(1, 1)
    else:
        p_kb, p_kk = _each(_mm_nt, khs, bb), _each(_mm_nt, khs, kk)
        p_rb, p_rk = _each(_mm_nt, rts, bb), _each(_mm_nt, rts, kk)
    lmat = [jnp.where(strict, x, 0.0) for x in p_kb]
    a_kk = [jnp.where(strict, x, 0.0) for x in p_kk]
    a_rb = [jnp.where(incl, x, 0.0) for x in p_rb]
    a_rk = [jnp.where(incl, x, 0.0) for x in p_rk]
    st = [st_sc[i] for i in range(len(problems))]
    vs = _each(_stack, v)
    from_state = _each(lambda x, y, s: _mm_nt(cat(x, y), s), kh, rt, st)
    from_v = _each(lambda a1, a2, y: _mm(cat(a1, a2), y), a_kk, a_rk, vs)
    rhs = _each(lambda s, a: _stack(s[:c]) + a[:n], from_state, from_v)
    y0 = _each(lambda s, a: _stack(s[c:]) + a[n:], from_state, from_v)
    tinv = _unit_lower_inverses(lmat, c)
    u = _each(_mm, tinv, rhs)
    ys = _each(lambda y, a, x: y - _mm(a, x), y0, a_rb, u)
    vu = _each(lambda x, y: jnp.concatenate([x, y], axis=0), vs, u)
    kb = _each(lambda x, y: jnp.concatenate([_stack(x), -_stack(y)], axis=0), k_end, b_end)
    st_new = _each(lambda s, xe, x, y: s * jnp.exp(xe) + _mm(x.T, y), st, cs_end, vu, kb)
    for i, (g, p) in enumerate(problems):
        y_ref[g, 0, :, sls[p]] = ys[i][:c] + ys[i][c:]
        st_sc[i] = st_new[i]

    @pl.when(ci == pl.num_programs(1) - 1)
    def _():
        for i, (g, p) in enumerate(problems):
            st = st_sc[i]
            sout_ref[g, 2 * p] = st[:RW_HEAD, :RW_HEAD]
            sout_ref[g, 2 * p + 1] = st[RW_HEAD:, RW_HEAD:]


def _wkv_scan(r, lw, k, v, kap, b, s0, batch, seq, c, nseq=WKV_SEQS):
    nc = seq // c
    nseq = math.gcd(nseq, batch)
    tile = pl.BlockSpec((nseq, 1, c, RW_W), lambda bi, ci: (bi, ci, 0, 0))
    sspec = pl.BlockSpec((nseq, RW_HEADS, RW_HEAD, RW_HEAD), lambda bi, ci: (bi, 0, 0, 0))
    tri = jnp.tril(jnp.ones((c, c), F32)).astype(BF16)
    tiles = [t.reshape(batch, nc, c, RW_W) for t in (r, lw, k, v, kap, b)]
    y, s_new = pl.pallas_call(
        functools.partial(_wkv_scan_kernel, c=c, nseq=nseq),
        grid=(batch // nseq, nc),
        in_specs=[tile] * 6 + [sspec, _full((c, c))],
        out_specs=[tile, sspec],
        out_shape=[jax.ShapeDtypeStruct((batch, nc, c, RW_W), F32),
                   jax.ShapeDtypeStruct((batch, RW_HEADS, RW_HEAD, RW_HEAD), F32)],
        scratch_shapes=[pltpu.VMEM((nseq * RW_PAIRS, LANES, LANES), F32)],
        compiler_params=_cparams(("parallel", "arbitrary")),
        name="wkv_scan",
    )(*tiles, s0, tri)
    return y.reshape(batch * seq, RW_W), s_new


def _ffn_kernel(x_ref, att_ref, y_ref, bonus_ref, g_ref, lnw_ref, lnb_ref, seg_ref,
                woa_ref, wor_ref, gmix_ref, gpre_ref, gpost_ref, wg_ref, wu_ref, wd_ref, o_ref):
    y = y_ref[...]
    seg = seg_ref[...]
    mean = _seg_sum(y, seg) * (1.0 / RW_HEAD)
    d = y - mean
    var = _seg_sum(d * d, seg) * (1.0 / RW_HEAD)
    yn = d * lax.rsqrt(var + GN_EPS) * lnw_ref[...] + lnb_ref[...]
    rw_out = (yn + bonus_ref[...]) * g_ref[...]
    mix = _mm(att_ref[...], woa_ref[...]) + _mm(rw_out, wor_ref[...])
    x = x_ref[...] + _rms(mix, gmix_ref[...], RMS_EPS)
    f = _rms(x, gpre_ref[...], RMS_EPS).astype(BF16)
    dot = functools.partial(jnp.dot, preferred_element_type=F32)
    tff = D_FF // FFN_CHUNKS
    acc = None
    for j in range(FFN_CHUNKS):
        cs = slice(j * tff, (j + 1) * tff)
        gate = dot(f, wg_ref[:, cs])
        up = dot(f, wu_ref[:, cs])
        h = gate * _sigmoid(gate) * up
        part = dot(h.astype(BF16), wd_ref[cs, :])
        acc = part if acc is None else acc + part
    o_ref[...] = x + _rms(acc, gpost_ref[...], RMS_EPS)


def _mix_ffn(x, att, y, bonus, g, p, tm):
    m = x.shape[0]
    row = lambda w: pl.BlockSpec((tm, w), lambda i: (i, 0))
    vec = _const((1, RW_W))
    wide = _const((1, D_MODEL))
    return pl.pallas_call(
        _ffn_kernel,
        grid=(m // tm,),
        in_specs=[row(D_MODEL), row(ATT_W), row(RW_W), row(RW_W), row(RW_W), vec, vec,
                  _const((RW_W, RW_W)), _const((ATT_W, D_MODEL)), _const((RW_W, D_MODEL)),
                  wide, wide, wide,
                  _const((D_MODEL, D_FF)), _const((D_MODEL, D_FF)), _const((D_FF, D_MODEL))],
        out_specs=row(D_MODEL),
        out_shape=jax.ShapeDtypeStruct((m, D_MODEL), F32),
        compiler_params=_cparams(("parallel",)),
        name="mix_ffn",
    )(x, att, y, bonus, g, p['lnx_w'], p['lnx_b'], p['seg'], p['w_out_att'], p['w_out_rw'],
      p['norm_mix_post'], p['norm_ffn_pre'], p['norm_ffn_post'],
      p['ffn_gate'], p['ffn_up'], p['ffn_down'])


LANE_ROWS = 8


def _wkv_lanes_kernel(r_ref, lw_ref, k_ref, v_ref, kap_ref, b_ref, s_ref, y_ref, sout_ref,
                      xt_sc, y_sc, *, n_tok):
    batch = s_ref.shape[-1]
    for a, ref in enumerate((r_ref, lw_ref, k_ref, v_ref, kap_ref, b_ref)):
        for t in range(n_tok):
            x = ref[pl.ds(t, batch, stride=n_tok), :].T
            xt_sc[a, t] = jnp.exp(x) if a == 1 else x

    for hd in range(2):
        ch = slice(hd * RW_HEAD, (hd + 1) * RW_HEAD)

        def rows(blk, carry):
            for u in range(LANE_ROWS):
                i = blk * LANE_ROWS + u
                s = s_ref[hd, i]
                for t in range(n_tok):
                    v_i = xt_sc[3, t, pl.ds(hd * RW_HEAD + i, 1), :]
                    sk = jnp.sum(s * xt_sc[4, t, ch, :], axis=0, keepdims=True)
                    s = s * xt_sc[1, t, ch, :] - sk * xt_sc[5, t, ch, :] + v_i * xt_sc[2, t, ch, :]
                    y_sc[t, pl.ds(hd * RW_HEAD + i, 1), :] = jnp.sum(
                        s * xt_sc[0, t, ch, :], axis=0, keepdims=True)
                sout_ref[hd, i] = s
            return carry

        lax.fori_loop(0, RW_HEAD // LANE_ROWS, rows, 0)

    for t in range(n_tok):
        y_ref[pl.ds(t, batch, stride=n_tok), :] = y_sc[t].T


def _wkv_lanes(r, lw, k, v, kap, b, s0, batch, seq):
    st = jnp.transpose(s0, (1, 2, 3, 0))
    col = pl.BlockSpec((batch * seq, LANES), lambda p: (0, p))
    sspec = pl.BlockSpec((2, RW_HEAD, RW_HEAD, batch), lambda p: (p, 0, 0, 0))
    y, st_new = pl.pallas_call(
        functools.partial(_wkv_lanes_kernel, n_tok=seq),
        grid=(RW_PAIRS,),
        in_specs=[col] * 6 + [sspec],
        out_specs=[col, sspec],
        out_shape=[jax.ShapeDtypeStruct((batch * seq, RW_W), F32),
                   jax.ShapeDtypeStruct((RW_HEADS, RW_HEAD, RW_HEAD, batch), F32)],
        scratch_shapes=[pltpu.VMEM((6, seq, LANES, batch), F32), pltpu.VMEM((seq, LANES, batch), F32)],
        compiler_params=_cparams(("parallel",)),
        name="wkv_lanes",
    )(r, lw, k, v, kap, b, st)
    return y, jnp.transpose(st_new, (3, 0, 1, 2))


def _layer_params(l, w):
    row = lambda a: a[l].reshape(1, -1).astype(F32)
    zeros64 = jnp.zeros((64, RW_W), F32)
    head = jnp.arange(RW_W) // RW_HEAD
    return {
        'norm_mix_pre': row(w['norm_mix_pre']), 'norm_mix_post': row(w['norm_mix_post']),
        'norm_ffn_pre': row(w['norm_ffn_pre']), 'norm_ffn_post': row(w['norm_ffn_post']),
        'w_in': w['w_in'][l].astype(BF16),
        'w_out_att': w['w_out'][l][:ATT_W].astype(BF16),
        'w_out_rw': w['w_out'][l][ATT_W:].astype(BF16),
        'lam': [row(w[n]) for n in ('lambda_q1', 'lambda_k1', 'lambda_q2', 'lambda_k2')],
        'subln_w': row(w['subln_w']),
        'mu': row(w['rw_mu']), 'w0': row(w['rw_w0']), 'a0': row(w['rw_a0']),
        'kk': row(w['rw_kk']), 'ka': row(w['rw_ka']), 'rk': row(w['rw_rk']),
        'lnx_w': row(w['rw_lnx_w']), 'lnx_b': row(w['rw_lnx_b']),
        'w2pad': jnp.concatenate([w['rw_w2'][l], zeros64], axis=0).astype(BF16),
        'a2pad': jnp.concatenate([zeros64, w['rw_a2'][l]], axis=0).astype(BF16),
        'g2': w['rw_g2'][l].astype(BF16),
        'seg': (head[:, None] == head[None, :]).astype(BF16),
        'ffn_gate': w['ffn_gate'][l].astype(BF16), 'ffn_up': w['ffn_up'][l].astype(BF16),
        'ffn_down': w['ffn_down'][l].astype(BF16),
    }


def _layer(x, shift0, wkv0, attend, p, batch, seq):
    m = batch * seq
    if seq >= WKV_CHUNK:
        tm = min(ROW_TILE, seq)
        tiles = seq // tm
        first = jnp.zeros((batch, tiles, RW_PROJ_W), F32).at[:, 0, :].set(shift0).reshape(m // tm, 1, RW_PROJ_W)
        period, c = tm, WKV_CHUNK
    else:
        tm = min(ROW_TILE, m)
        tiles = 1
        first = shift0.reshape(m // tm, tm // seq, RW_PROJ_W)
        period, c = seq, 8
    q, k, v, k4, v4, r, lw, k2, vr, kap, b, gate, bonus, tail = _inproj(x, first, period, tiles, p, tm)
    att = attend(q, k, v)
    scan_in = (r, lw, k2, vr, kap, b)
    if c > seq and batch % LANES == 0:
        y, wkv_new = _wkv_lanes(*scan_in, wkv0, batch, seq)
        shift_new = tail[seq - 1::seq]
    elif c > seq:
        pad = lambda t: jnp.pad(t.reshape(batch, seq, RW_W), ((0, 0), (0, c - seq), (0, 0))).reshape(batch * c, RW_W)
        y, wkv_new = _wkv_scan(*[pad(t) for t in scan_in], wkv0, batch, c, c)
        y = y.reshape(batch, c, RW_W)[:, :seq].reshape(m, RW_W)
        shift_new = tail.reshape(batch, seq, RW_PROJ_W)[:, -1]
    else:
        y, wkv_new = _wkv_scan(*scan_in, wkv0, batch, seq, c)
        shift_new = tail.reshape(batch, tiles, RW_PROJ_W)[:, -1]
    out = _mix_ffn(x, att, y, bonus, gate, p, tm)
    return out, k4, v4, wkv_new, shift_new


def kernel(x_prompt, x_sample, cache_k, cache_v, state_wkv, state_shift, page_table, norm_mix_pre, norm_mix_post, norm_ffn_pre, norm_ffn_post, w_in, w_out, lambda_q1, lambda_k1, lambda_q2, lambda_k2, subln_w, rw_mu, rw_w0, rw_w2, rw_a0, rw_a2, rw_g2, rw_kk, rw_ka, rw_rk, rw_lnx_w, rw_lnx_b, ffn_gate, ffn_up, ffn_down):
    weights = dict(
        norm_mix_pre=norm_mix_pre, norm_mix_post=norm_mix_post, norm_ffn_pre=norm_ffn_pre,
        norm_ffn_post=norm_ffn_post, w_in=w_in, w_out=w_out, lambda_q1=lambda_q1,
        lambda_k1=lambda_k1, lambda_q2=lambda_q2, lambda_k2=lambda_k2, subln_w=subln_w,
        rw_mu=rw_mu, rw_w0=rw_w0, rw_w2=rw_w2, rw_a0=rw_a0, rw_a2=rw_a2, rw_g2=rw_g2,
        rw_kk=rw_kk, rw_ka=rw_ka, rw_rk=rw_rk, rw_lnx_w=rw_lnx_w, rw_lnx_b=rw_lnx_b,
        ffn_gate=ffn_gate, ffn_up=ffn_up, ffn_down=ffn_down)
    depth = w_in.shape[0]
    batch, seq, _ = x_prompt.shape
    db, dseq, _ = x_sample.shape
    n_pool, page = cache_k.shape[1], cache_k.shape[2]
    yp = x_prompt.reshape(batch * seq, D_MODEL)
    ys = x_sample.reshape(db * dseq, D_MODEL)
    outs = [[] for _ in range(8)]
    for l in range(depth):
        lam_init = 0.8 - 0.6 * math.exp(-0.3 * l)
        p = _layer_params(l, weights)
        ck = cache_k[l].reshape(n_pool, page * ATT_HEADS, LANES)
        cv = cache_v[l].reshape(n_pool, page * ATT_HEADS, LANES)

        def attend_p(q, k, v):
            return _attn_prompt(q, k, v, p['lam'], p['subln_w'], batch, seq, lam_init)

        def attend_s(q, k, v):
            t3 = lambda a: a.reshape(db, dseq, ATT_W)
            o = _attn_paged(t3(q), t3(k), t3(v), ck, cv, page_table, p['lam'], p['subln_w'], lam_init)
            return o.reshape(db * dseq, ATT_W)

        zshift = jnp.zeros((batch, RW_PROJ_W), F32)
        zwkv = jnp.zeros((batch, RW_HEADS, RW_HEAD, RW_HEAD), F32)
        yp, kp, vp, wp, sp = _layer(yp, zshift, zwkv, attend_p, p, batch, seq)
        ys, ks, vs, ws, ss = _layer(ys, state_shift[l], state_wkv[l], attend_s, p, db, dseq)
        kv_p = lambda a: a.reshape(batch, seq, ATT_HEADS, ATT_V_DIM)
        kv_s = lambda a: a.reshape(db, dseq, ATT_HEADS, ATT_V_DIM)
        for lst, val in zip(outs, (kv_p(kp), kv_p(vp), wp, sp, kv_s(ks), kv_s(vs), ws, ss)):
            lst.append(val)
    stacked = [jnp.stack(o) for o in outs]
    return (yp.reshape(batch, seq, D_MODEL), ys.reshape(db, dseq, D_MODEL), *stacked)
```
